```python
import math
import jax, jax.numpy as jnp
from jax import lax
import numpy as np

D_MODEL = 2048
BATCH = 16
SEQ = 2048
DEPTH = 1

GRID_W = 64
CTX_LEN = 256
N_MOD = 9
D_FF = 5504
A_HEADS = 8
A_HEAD_DIM = 128
A_DIM = A_HEADS * A_HEAD_DIM
A_CHUNK = 128
ROWS_PER_CHUNK = A_CHUNK // GRID_W
SSD_HEADS = 16
SSD_HEAD_DIM = 64
SSD_DIM = SSD_HEADS * SSD_HEAD_DIM
SSD_GROUPS = 2
SSD_HPG = SSD_HEADS // SSD_GROUPS
SSD_STATE = 128
SSD_CHUNK = 128
CONV_W = 5
XBC_DIM = SSD_DIM + 2 * SSD_GROUPS * SSD_STATE
MIX_DIM = A_DIM + SSD_DIM
IN_COLS = 2 * A_DIM + SSD_DIM + XBC_DIM + 2 * SSD_HEADS
EPS = 1e-6

kernel_name = 'hybrid_gmlp_ssd_macaron_prefix_block'


def _rms_norm(x, g):
    xf = x.astype(jnp.float32)
    y = xf * lax.rsqrt(jnp.mean(xf * xf, axis=-1, keepdims=True) + EPS)
    return (y * g.astype(jnp.float32)).astype(x.dtype)


def _layer_norm(x, g, b):
    xf = x.astype(jnp.float32)
    mu = jnp.mean(xf, axis=-1, keepdims=True)
    xc = xf - mu
    y = xc * lax.rsqrt(jnp.mean(xc * xc, axis=-1, keepdims=True) + EPS)
    return (y * g.astype(jnp.float32) + b.astype(jnp.float32)).astype(x.dtype)


def _modulate(h, shift, scale):
    return h * (1.0 + scale) + shift


def _half_ffn(x, mod, k, g, w_gate, w_up, w_down):
    h = _modulate(_rms_norm(x, g), mod[:, :, 3 * k], mod[:, :, 3 * k + 1])
    f = (jax.nn.silu(h @ w_gate) * (h @ w_up)) @ w_down
    return x + 0.5 * mod[:, :, 3 * k + 2] * f


def _dwconv_centred(x, w, b):
    ch = x.shape[-1]
    y = lax.conv_general_dilated(x, w[:, None, :].astype(x.dtype), window_strides=(1,),
                                 padding=[(CONV_W // 2, CONV_W // 2)],
                                 dimension_numbers=('NWC', 'WIO', 'NWC'),
                                 feature_group_count=ch)
    return y + b


def _chunk_mlp(u, v, n_chunks, g, beta, w_s, b_s):
    bsz = u.shape[0]
    v = _layer_norm(jax.nn.gelu(v, approximate=False), g, beta)
    vc = v.reshape(bsz, n_chunks, A_CHUNK, A_HEADS, A_HEAD_DIM)
    s = jnp.einsum('hqk,bckhd->bcqhd', w_s, vc) + b_s.T[:, :, None]
    return jax.nn.gelu(u, approximate=False) * s.reshape(bsz, n_chunks * A_CHUNK, A_DIM)


def _ssd_direction(xs, dt, A, Bm, Cm, h0, with_output):
    bsz, l, _, _ = xs.shape
    nc = l // SSD_CHUNK
    xdt = (xs * dt[..., None]).reshape(bsz, nc, SSD_CHUNK, SSD_GROUPS, SSD_HPG, SSD_HEAD_DIM)
    a_cum = jnp.cumsum((dt * A).reshape(bsz, nc, SSD_CHUNK, SSD_GROUPS, SSD_HPG), axis=2)
    Bc = Bm.reshape(bsz, nc, SSD_CHUNK, SSD_GROUPS, SSD_STATE)
    to_end = jnp.exp(a_cum[:, :, -1:] - a_cum)
    chunk_states = jnp.einsum('bcqgn,bcqgjp->bcgjpn', Bc, xdt * to_end[..., None])
    chunk_decay = jnp.exp(a_cum[:, :, -1])

    def step(h, inp):
        s, d = inp
        return d[..., None, None] * h + s, h

    final, prev = lax.scan(step, h0, (jnp.moveaxis(chunk_states, 1, 0),
                                      jnp.moveaxis(chunk_decay, 1, 0)))
    if not with_output:
        return None, final
    Cc = Cm.reshape(bsz, nc, SSD_CHUNK, SSD_GROUPS, SSD_STATE)
    cb = jnp.einsum('bcqgn,bckgn->bcgqk', Cc, Bc)
    seg = a_cum[:, :, :, None] - a_cum[:, :, None]
    mask = jnp.tril(jnp.ones((SSD_CHUNK, SSD_CHUNK), dtype=bool))[:, :, None, None]
    decay = jnp.where(mask, jnp.exp(jnp.where(mask, seg, 0.0)), 0.0)
    y_diag = jnp.einsum('bcgqk,bcqkgj,bckgjp->bcqgjp', cb, decay, xdt)
    y_off = jnp.einsum('bcqgn,cbgjpn->bcqgjp', Cc, prev) * jnp.exp(a_cum)[..., None]
    return (y_diag + y_off).reshape(bsz, l, SSD_HEADS, SSD_HEAD_DIM), final


def _flip(t):
    return jnp.flip(t, axis=1)


def _mixer(p, n_chunks, h0_f, h0_b, with_output, conv_w, conv_b, dt_bias, a_log, d_skip,
           ssd_norm_g, gmlp_norm_g, gmlp_norm_b, gmlp_w_s, gmlp_b_s, w_out):
    bsz, l, _ = p.shape
    o1 = 2 * A_DIM
    o2 = o1 + SSD_DIM
    o3 = o2 + XBC_DIM
    xbc = jax.nn.silu(_dwconv_centred(p[..., o2:o3], conv_w, conv_b)).astype(jnp.float32)
    xs = xbc[..., :SSD_DIM].reshape(bsz, l, SSD_HEADS, SSD_HEAD_DIM)
    gn = SSD_GROUPS * SSD_STATE
    Bm = xbc[..., SSD_DIM:SSD_DIM + gn].reshape(bsz, l, SSD_GROUPS, SSD_STATE)
    Cm = xbc[..., SSD_DIM + gn:].reshape(bsz, l, SSD_GROUPS, SSD_STATE)
    dt = jax.nn.softplus(p[..., o3:].astype(jnp.float32).reshape(bsz, l, 2, SSD_HEADS)
                         + dt_bias.astype(jnp.float32))
    A = -jnp.exp(a_log.astype(jnp.float32))
    y_f, fin_f = _ssd_direction(xs, dt[:, :, 0], A[0], Bm, Cm, h0_f, with_output)
    y_b, fin_b = _ssd_direction(_flip(xs), _flip(dt[:, :, 1]), A[1], _flip(Bm), _flip(Cm),
                                h0_b, with_output)
    if not with_output:
        return None, fin_f, fin_b
    y = (y_f + _flip(y_b) + d_skip.astype(jnp.float32)[:, None] * xs).reshape(bsz, l, SSD_DIM)
    yz = (y * jax.nn.silu(p[..., o1:o2].astype(jnp.float32))).reshape(bsz, l, SSD_GROUPS, -1)
    yz = yz * lax.rsqrt(jnp.mean(yz * yz, axis=-1, keepdims=True) + EPS)
    ssd_out = (yz.reshape(bsz, l, SSD_DIM) * ssd_norm_g.astype(jnp.float32)).astype(p.dtype)
    a_out = _chunk_mlp(p[..., :A_DIM], p[..., A_DIM:o1], n_chunks, gmlp_norm_g, gmlp_norm_b,
                       gmlp_w_s, gmlp_b_s)
    out = jnp.concatenate([a_out, ssd_out], axis=-1) @ w_out
    return out, fin_f, fin_b


def setup_inputs(seed: int = 0) -> dict:
    key = jax.random.key(seed)
    ks = jax.random.split(key, 32)

    def nrm(k, shape, scale):
        return jax.random.normal(k, shape, jnp.float32) * scale

    def gain(k, shape):
        return 1.0 + 0.02 * jax.random.normal(k, shape, jnp.float32)

    dt0 = jnp.exp(jax.random.uniform(ks[16], (DEPTH, 2, SSD_HEADS), jnp.float32,
                                     minval=math.log(1e-3), maxval=math.log(1e-1)))
    return {
        'x': nrm(ks[0], (BATCH, SEQ, D_MODEL), 1.0),
        'c': nrm(ks[1], (BATCH, D_MODEL), 1.0),
        'ctx': nrm(ks[2], (BATCH, CTX_LEN, D_MODEL), 1.0),
        'c_ctx': nrm(ks[3], (D_MODEL,), 1.0),
        'w_mod': nrm(ks[4], (DEPTH, D_MODEL, N_MOD * D_MODEL), 0.5 * D_MODEL ** -0.5),
        'b_mod': nrm(ks[5], (DEPTH, N_MOD * D_MODEL), 0.02),
        'norm_ffn1': gain(ks[6], (DEPTH, D_MODEL)),
        'ffn1_w_gate': nrm(ks[7], (DEPTH, D_MODEL, D_FF), D_MODEL ** -0.5),
        'ffn1_w_up': nrm(ks[8], (DEPTH, D_MODEL, D_FF), D_MODEL ** -0.5),
        'ffn1_w_down': nrm(ks[9], (DEPTH, D_FF, D_MODEL), D_FF ** -0.5),
        'norm_mix': gain(ks[10], (DEPTH, D_MODEL)),
        'w_in': nrm(ks[11], (DEPTH, D_MODEL, IN_COLS), D_MODEL ** -0.5),
        'conv_w': nrm(ks[12], (DEPTH, CONV_W, XBC_DIM), CONV_W ** -0.5),
        'conv_b': nrm(ks[13], (DEPTH, XBC_DIM), 0.02),
        'dt_bias': dt0 + jnp.log(-jnp.expm1(-dt0)),
        'a_log': jnp.log(jax.random.uniform(ks[14], (DEPTH, 2, SSD_HEADS), jnp.float32,
                                            minval=1.0, maxval=16.0)),
        'd_skip': gain(ks[15], (DEPTH, SSD_HEADS)),
        'ssd_norm_g': gain(ks[17], (DEPTH, SSD_DIM)),
        'gmlp_norm_g': gain(ks[18], (DEPTH, A_DIM)),
        'gmlp_norm_b': nrm(ks[19], (DEPTH, A_DIM), 0.02),
        'gmlp_w_s': nrm(ks[20], (DEPTH, A_HEADS, A_CHUNK, A_CHUNK), A_CHUNK ** -0.5),
        'gmlp_b_s': gain(ks[21], (DEPTH, A_HEADS, A_CHUNK)),
        'w_out': nrm(ks[22], (DEPTH, MIX_DIM, D_MODEL), MIX_DIM ** -0.5),
        'norm_ffn2': gain(ks[23], (DEPTH, D_MODEL)),
        'ffn2_w_gate': nrm(ks[24], (DEPTH, D_MODEL, D_FF), D_MODEL ** -0.5),
        'ffn2_w_up': nrm(ks[25], (DEPTH, D_MODEL, D_FF), D_MODEL ** -0.5),
        'ffn2_w_down': nrm(ks[26], (DEPTH, D_FF, D_MODEL), D_FF ** -0.5),
        'norm_final': gain(ks[27], (D_MODEL,)),
    }


def reference(x, c, ctx, c_ctx, w_mod, b_mod, norm_ffn1, ffn1_w_gate, ffn1_w_up, ffn1_w_down,
              norm_mix, w_in, conv_w, conv_b, dt_bias, a_log, d_skip, ssd_norm_g,
              gmlp_norm_g, gmlp_norm_b, gmlp_w_s, gmlp_b_s, w_out, norm_ffn2, ffn2_w_gate,
              ffn2_w_up, ffn2_w_down, norm_final):
    bsz = x.shape[0]
    rows = x.shape[1] // GRID_W
    lat_chunks = rows // ROWS_PER_CHUNK
    ctx_chunks = ctx.shape[1] // A_CHUNK
    h0 = jnp.zeros((bsz, SSD_GROUPS, SSD_HPG, SSD_HEAD_DIM, SSD_STATE), jnp.float32)
    for i in range(DEPTH):
        last = i == DEPTH - 1
        mod_x = (jax.nn.silu(c) @ w_mod[i] + b_mod[i]).reshape(bsz, 1, N_MOD, D_MODEL)
        mod_c = (jax.nn.silu(c_ctx) @ w_mod[i] + b_mod[i]).reshape(1, 1, N_MOD, D_MODEL)
        x = _half_ffn(x, mod_x, 0, norm_ffn1[i], ffn1_w_gate[i], ffn1_w_up[i], ffn1_w_down[i])
        ctx = _half_ffn(ctx, mod_c, 0, norm_ffn1[i], ffn1_w_gate[i], ffn1_w_up[i], ffn1_w_down[i])
        px = _modulate(_rms_norm(x, norm_mix[i]), mod_x[:, :, 3], mod_x[:, :, 4]) @ w_in[i]
        pc = _modulate(_rms_norm(ctx, norm_mix[i]), mod_c[:, :, 3], mod_c[:, :, 4]) @ w_in[i]
        mp = (conv_w[i], conv_b[i], dt_bias[i], a_log[i], d_skip[i], ssd_norm_g[i],
              gmlp_norm_g[i], gmlp_norm_b[i], gmlp_w_s[i], gmlp_b_s[i], w_out[i])
        out_c, h_f, h_b = _mixer(pc, ctx_chunks, h0, h0, not last, *mp)
        out_x, _, _ = _mixer(px, lat_chunks, h_f, h_b, True, *mp)
        x = x + mod_x[:, :, 5] * out_x
        x = _half_ffn(x, mod_x, 2, norm_ffn2[i], ffn2_w_gate[i], ffn2_w_up[i], ffn2_w_down[i])
        if not last:
            ctx = ctx + mod_c[:, :, 5] * out_c
            ctx = _half_ffn(ctx, mod_c, 2, norm_ffn2[i], ffn2_w_gate[i], ffn2_w_up[i],
                            ffn2_w_down[i])
    return _rms_norm(x, norm_final)
```

```python
import functools
import math

import jax
import jax.numpy as jnp
import numpy as np
from jax import lax
from jax.experimental import pallas as pl
from jax.experimental.pallas import tpu as pltpu

F32 = jnp.float32
BF16 = jnp.bfloat16

D_MODEL = 2048
N_MOD = 9
D_FF = 5504
A_HEADS = 8
A_HEAD_DIM = 128
A_DIM = A_HEADS * A_HEAD_DIM
SSD_HEADS = 16
SSD_HEAD_DIM = 64
SSD_DIM = SSD_HEADS * SSD_HEAD_DIM
SSD_GROUPS = 2
SSD_STATE = 128
GROUP_DIM = SSD_DIM // SSD_GROUPS
CHUNK = 128
CONV_W = 5
XBC_DIM = SSD_DIM + 2 * SSD_GROUPS * SSD_STATE
MAIN_COLS = 2 * A_DIM + SSD_DIM + XBC_DIM
EPS = 1e-6

LANES = 128
SUBLANES = 8
MXU_DIM = 256
VMEM_LIMIT_BYTES = 60 * 1024 * 1024

D_FF_PAD = ((D_FF + 2 * MXU_DIM - 1) // (2 * MXU_DIM)) * (2 * MXU_DIM)
HALO = SUBLANES
PAD_ROWS = CHUNK + 2 * HALO


def _params(sem):
    return pltpu.CompilerParams(dimension_semantics=sem, vmem_limit_bytes=VMEM_LIMIT_BYTES)


def _dot(a, b):
    return jnp.dot(a, b, preferred_element_type=F32)


def _silu(x):
    return x * jax.nn.sigmoid(x)


def _gelu(x):
    return 0.5 * x * (1.0 + lax.erf(x * (1.0 / math.sqrt(2.0))))


def _softplus(x):
    return jnp.maximum(x, 0.0) + jnp.log1p(jnp.exp(-jnp.abs(x)))


def _rms(x, g):
    return x * lax.rsqrt(jnp.mean(x * x, axis=-1, keepdims=True) + EPS) * g


def _mod_kernel(c_ref, w_ref, b_ref, o_ref):
    o_ref[...] = _dot(_silu(c_ref[...]), w_ref[...]) + b_ref[...]


def _modulation(c_all, w_mod, b_mod, tn=1024):
    rows = c_all.shape[0]
    n = w_mod.shape[1]
    return pl.pallas_call(
        _mod_kernel,
        grid=(n // tn,),
        in_specs=[pl.BlockSpec((rows, D_MODEL), lambda j: (0, 0)),
                  pl.BlockSpec((D_MODEL, tn), lambda j: (0, j)),
                  pl.BlockSpec((1, tn), lambda j: (0, j))],
        out_specs=pl.BlockSpec((rows, tn), lambda j: (0, j)),
        out_shape=jax.ShapeDtypeStruct((rows, n), F32),
        compiler_params=_params(("arbitrary",)),
        name="mod",
    )(c_all, w_mod, b_mod.reshape(1, n))


def _ffn_kernel(x_ref, shift_ref, scale_ref, gate_ref, g_ref, wg_ref, wu_ref, wd_ref, gf_ref,
                o_ref, h_ref, *, final_norm):
    j = pl.program_id(1)

    @pl.when(j == 0)
    def _():
        h = _rms(x_ref[...], g_ref[...]) * (1.0 + scale_ref[...]) + shift_ref[...]
        h_ref[...] = h.astype(BF16)
        o_ref[...] = jnp.zeros_like(o_ref)

    h = h_ref[...]
    gate = _dot(h, wg_ref[...])
    up = _dot(h, wu_ref[...])
    a = (_silu(gate) * up).astype(BF16)
    o_ref[...] += _dot(a, wd_ref[...])

    @pl.when(j == pl.num_programs(1) - 1)
    def _():
        r = x_ref[...] + (0.5 * gate_ref[...]) * o_ref[...]
        if final_norm:
            r = _rms(r, gf_ref[...])
        o_ref[...] = r


def _ffn(x2d, shift, scale, gate, g, wg, wu, wd, gf, *, rows_per_mod, final_norm, tm=512, tf=512):
    t = x2d.shape[0]
    tiles_per_mod = rows_per_mod // tm
    mod_spec = pl.BlockSpec((None, 1, D_MODEL), lambda i, j: (i // tiles_per_mod, 0, 0))
    row_spec = pl.BlockSpec((1, D_MODEL), lambda i, j: (0, 0))
    tile_spec = pl.BlockSpec((tm, D_MODEL), lambda i, j: (i, 0))
    return pl.pallas_call(
        functools.partial(_ffn_kernel, final_norm=final_norm),
        grid=(t // tm, D_FF_PAD // tf),
        in_specs=[tile_spec, mod_spec, mod_spec, mod_spec, row_spec,
                  pl.BlockSpec((D_MODEL, tf), lambda i, j: (0, j)),
                  pl.BlockSpec((D_MODEL, tf), lambda i, j: (0, j)),
                  pl.BlockSpec((tf, D_MODEL), lambda i, j: (j, 0)),
                  row_spec],
        out_specs=tile_spec,
        out_shape=jax.ShapeDtypeStruct((t, D_MODEL), F32),
        scratch_shapes=[pltpu.VMEM((tm, D_MODEL), BF16)],
        compiler_params=_params(("arbitrary", "arbitrary")),
        name="ffn_final" if final_norm else "ffn",
    )(x2d, shift, scale, gate, g, wg, wu, wd, gf)


def _inproj_kernel(x_ref, shift_ref, scale_ref, g_ref, w_ref, wdt_ref, p_ref, dt_ref, h_ref):
    @pl.when(pl.program_id(1) == 0)
    def _():
        h = _rms(x_ref[...], g_ref[...]) * (1.0 + scale_ref[...]) + shift_ref[...]
        h_ref[...] = h.astype(BF16)
        dt_ref[...] = _dot(h_ref[...], wdt_ref[...])

    p_ref[...] = _dot(h_ref[...], w_ref[...])


def _inproj(x2d, shift, scale, g, w, wdt, *, rows_per_mod, tm=1024, tn=XBC_DIM):
    t = x2d.shape[0]
    n = w.shape[1]
    tiles_per_mod = rows_per_mod // tm
    mod_spec = pl.BlockSpec((None, 1, D_MODEL), lambda i, j: (i // tiles_per_mod, 0, 0))
    return pl.pallas_call(
        _inproj_kernel,
        grid=(t // tm, n // tn),
        in_specs=[pl.BlockSpec((tm, D_MODEL), lambda i, j: (i, 0)), mod_spec, mod_spec,
                  pl.BlockSpec((1, D_MODEL), lambda i, j: (0, 0)),
                  pl.BlockSpec((D_MODEL, tn), lambda i, j: (0, j)),
                  pl.BlockSpec((D_MODEL, LANES), lambda i, j: (0, 0))],
        out_specs=[pl.BlockSpec((tm, tn), lambda i, j: (i, j)),
                   pl.BlockSpec((tm, LANES), lambda i, j: (i, 0))],
        out_shape=[jax.ShapeDtypeStruct((t, n), F32), jax.ShapeDtypeStruct((t, LANES), F32)],
        scratch_shapes=[pltpu.VMEM((tm, D_MODEL), BF16)],
        compiler_params=_params(("arbitrary", "arbitrary")),
        name="inproj",
    )(x2d, shift, scale, g, w, wdt)


def _conv_silu(cur_ref, prev_ref, next_ref, w_ref, b_ref, pad_ref, is_first, is_last, ncols):
    pad_ref[0:HALO, 0:ncols] = jnp.where(is_first, 0.0, prev_ref[:, 0:ncols])
    pad_ref[HALO:HALO + CHUNK, 0:ncols] = cur_ref[:, 0:ncols]
    pad_ref[HALO + CHUNK:PAD_ROWS, 0:ncols] = jnp.where(is_last, 0.0, next_ref[:, 0:ncols])
    acc = jnp.broadcast_to(b_ref[:, 0:ncols], (CHUNK, ncols))
    for k in range(CONV_W):
        start = HALO - CONV_W // 2 + k
        acc = acc + pad_ref[start:start + CHUNK, 0:ncols] * w_ref[k:k + 1, 0:ncols]
    return _silu(acc)


def _dt_terms(dt_raw, bias_row, a_row, tl, tu):
    dt = _softplus(dt_raw + bias_row)
    a = dt * a_row
    s = jnp.dot(tl, a, precision=lax.Precision.HIGHEST, preferred_element_type=F32)
    r = jnp.dot(tu, a, precision=lax.Precision.HIGHEST, preferred_element_type=F32)
    is_fwd = lax.broadcasted_iota(jnp.int32, (CHUNK, LANES), 1) < SSD_HEADS
    cum = jnp.where(is_fwd, s, r)
    total = jnp.where(is_fwd[0:1], s[CHUNK - 1:CHUNK], r[0:1])
    return dt, cum, total


def _expand(v, e):
    hi = v.astype(BF16)
    lo = (v - hi.astype(F32)).astype(BF16)
    return _dot(hi, e) + _dot(lo, e)


def _scan_kernel(xcf_ref, xpf_ref, xnf_ref, dtf_ref, xcb_ref, xpb_ref, xnb_ref, dtb_ref,
                 h0f_ref, h0b_ref, cw_ref, cb_ref, bias_ref, arow_ref, tl_ref, tu_ref,
                 ef_ref, eb_ref, pf_ref, pb_ref, ff_ref, fb_ref, pad_ref, sf_ref, sb_ref, *, nc):
    c = pl.program_id(1)
    ncols = SSD_DIM + SSD_GROUPS * SSD_STATE

    @pl.when(c == 0)
    def _():
        sf_ref[...] = h0f_ref[...]
        sb_ref[...] = h0b_ref[...]

    def advance(xc_ref, xp_ref, xn_ref, dt_ref, is_first, is_last, e_ref, state_ref, prev_ref):
        xb = _conv_silu(xc_ref, xp_ref, xn_ref, cw_ref, cb_ref, pad_ref, is_first, is_last, ncols)
        dt, cum, total = _dt_terms(dt_ref[...], bias_ref[...], arow_ref[...], tl_ref[...], tu_ref[...])
        w = dt * jnp.exp(total - cum)
        decay = jnp.broadcast_to(jnp.exp(total), (SUBLANES, LANES))
        ex = _expand(jnp.concatenate([w, decay], axis=0), e_ref[...])
        xw = (xb[:, 0:SSD_DIM] * ex[0:CHUNK]).astype(BF16)
        state = state_ref[...]
        prev_ref[...] = state.astype(BF16)
        new = []
        for g in range(SSD_GROUPS):
            lo = SSD_DIM + g * SSD_STATE
            bg_t = xb[:, lo:lo + SSD_STATE].T.astype(BF16)
            new.append(_dot(bg_t, xw[:, g * GROUP_DIM:(g + 1) * GROUP_DIM]))
        state_ref[...] = state * ex[CHUNK:CHUNK + 1] + jnp.concatenate(new, axis=1)

    advance(xcf_ref, xpf_ref, xnf_ref, dtf_ref, c == 0, c == nc - 1, ef_ref, sf_ref, pf_ref)
    advance(xcb_ref, xpb_ref, xnb_ref, dtb_ref, c == nc - 1, c == 0, eb_ref, sb_ref, pb_ref)

    @pl.when(c == nc - 1)
    def _():
        ff_ref[...] = sf_ref[...]
        fb_ref[...] = sb_ref[...]


def _halo_specs(nc, col, rev):
    per = CHUNK // HALO

    def chunk_of(c):
        return nc - 1 - c if rev else c

    cur = pl.BlockSpec((CHUNK, XBC_DIM), lambda b, c: (b * nc + chunk_of(c), col))
    prev = pl.BlockSpec((HALO, XBC_DIM),
                        lambda b, c: (jnp.maximum((b * nc + chunk_of(c)) * per - 1, 0), col))
    nxt = pl.BlockSpec((HALO, XBC_DIM),
                       lambda b, c: (jnp.minimum((b * nc + chunk_of(c) + 1) * per,
                                                 (b * nc + nc) * per - 1), col))
    dt = pl.BlockSpec((CHUNK, LANES), lambda b, c: (b * nc + chunk_of(c), 0))
    return cur, prev, nxt, dt


def _full_spec(a):
    return pl.BlockSpec(a.shape, lambda b, c: (0,) * a.ndim)


def _scan(p, dt, h0f, h0b, consts, *, nb, nc, col):
    cw, cb, bias_row, a_row, tl, tu, ef, eb = consts
    fwd = _halo_specs(nc, col, False)
    bwd = _halo_specs(nc, col, True)
    state_spec = pl.BlockSpec((None, SSD_STATE, SSD_DIM), lambda b, c: (b, 0, 0))
    args = (p, p, p, dt, p, p, p, dt, h0f, h0b, cw, cb, bias_row, a_row, tl, tu, ef, eb)
    in_specs = list(fwd) + list(bwd) + [state_spec, state_spec] + [_full_spec(a) for a in args[10:]]
    return pl.pallas_call(
        functools.partial(_scan_kernel, nc=nc),
        grid=(nb, nc),
        in_specs=in_specs,
        out_specs=[pl.BlockSpec((None, SSD_STATE, SSD_DIM), lambda b, c: (b * nc + c, 0, 0)),
                   pl.BlockSpec((None, SSD_STATE, SSD_DIM), lambda b, c: (b * nc + nc - 1 - c, 0, 0)),
                   state_spec, state_spec],
        out_shape=[jax.ShapeDtypeStruct((nb * nc, SSD_STATE, SSD_DIM), BF16),
                   jax.ShapeDtypeStruct((nb * nc, SSD_STATE, SSD_DIM), BF16),
                   jax.ShapeDtypeStruct((nb, SSD_STATE, SSD_DIM), F32),
                   jax.ShapeDtypeStruct((nb, SSD_STATE, SSD_DIM), F32)],
        scratch_shapes=[pltpu.VMEM((PAD_ROWS, XBC_DIM), F32),
                        pltpu.VMEM((SSD_STATE, SSD_DIM), F32),
                        pltpu.VMEM((SSD_STATE, SSD_DIM), F32)],
        compiler_params=_params(("arbitrary", "arbitrary")),
        name="ssd_scan",
    )(*args)


def _mixer_kernel(u_ref, v_ref, z_ref, xc_ref, xp_ref, xn_ref, dt_ref, pf_ref, pb_ref,
                  cw_ref, cb_ref, bias_ref, arow_ref, tl_ref, tu_ref, ef_ref, eb_ref,
                  dskip_ref, ng_ref, lg_ref, lb_ref, ws_ref, bs_ref, o_ref, pad_ref, *, nc):
    c = pl.program_id(1)
    xbc = _conv_silu(xc_ref, xp_ref, xn_ref, cw_ref, cb_ref, pad_ref, c == 0, c == nc - 1, XBC_DIM)
    xs = xbc[:, 0:SSD_DIM]
    gn = SSD_GROUPS * SSD_STATE
    bm = xbc[:, SSD_DIM:SSD_DIM + gn].astype(BF16)
    cm = xbc[:, SSD_DIM + gn:XBC_DIM].astype(BF16)

    dt, cum, _ = _dt_terms(dt_ref[...], bias_ref[...], arow_ref[...], tl_ref[...], tu_ref[...])
    ecum = jnp.exp(cum)
    ecum_f = _expand(ecum, ef_ref[...])
    ecum_b = _expand(ecum, eb_ref[...])
    dt_t = dt.T
    cum_t = cum.T

    lane = lax.broadcasted_iota(jnp.int32, (CHUNK, SSD_DIM), 1)
    first_head = (lane % LANES) < SSD_HEAD_DIM
    xs_even = jnp.where(first_head, xs, 0.0).astype(BF16)
    xs_odd = jnp.where(first_head, 0.0, xs).astype(BF16)

    row = lax.broadcasted_iota(jnp.int32, (CHUNK, CHUNK), 0)
    col = lax.broadcasted_iota(jnp.int32, (CHUNK, CHUNK), 1)
    causal = row >= col
    anticausal = row <= col

    def decay_matrix(h):
        lf = jnp.exp(jnp.minimum(cum[:, h:h + 1] - cum_t[h:h + 1, :], 0.0))
        lf = jnp.where(causal, lf, 0.0) * dt_t[h:h + 1, :]
        hb = SSD_HEADS + h
        lb = jnp.exp(jnp.minimum(cum[:, hb:hb + 1] - cum_t[hb:hb + 1, :], 0.0))
        lb = jnp.where(anticausal, lb, 0.0) * dt_t[hb:hb + 1, :]
        return lf + lb

    pf = pf_ref[...]
    pb = pb_ref[...]
    heads_per_group = SSD_HEADS // SSD_GROUPS
    y_parts = []
    for g in range(SSD_GROUPS):
        cg = cm[:, g * SSD_STATE:(g + 1) * SSD_STATE]
        bg = bm[:, g * SSD_STATE:(g + 1) * SSD_STATE]
        cb_g = lax.dot_general(cg, bg, (((1,), (1,)), ((), ())), preferred_element_type=F32)
        gs = slice(g * GROUP_DIM, (g + 1) * GROUP_DIM)
        y_off = ecum_f[:, gs] * _dot(cg, pf[:, gs]) + ecum_b[:, gs] * _dot(cg, pb[:, gs])
        diag = []
        for j in range(heads_per_group // 2):
            h0 = g * heads_per_group + 2 * j
            m_pair = jnp.concatenate([(cb_g * decay_matrix(h0)).astype(BF16),
                                      (cb_g * decay_matrix(h0 + 1)).astype(BF16)], axis=1)
            ls = slice(h0 * SSD_HEAD_DIM, (h0 + 2) * SSD_HEAD_DIM)
            rhs = jnp.concatenate([xs_even[:, ls], xs_odd[:, ls]], axis=0)
            diag.append(_dot(m_pair, rhs))
        y_parts.append(jnp.concatenate(diag, axis=1) + y_off)
    y = jnp.concatenate(y_parts, axis=1) + dskip_ref[...] * xs

    yz = y * _silu(z_ref[...])
    normed = []
    for g in range(SSD_GROUPS):
        part = yz[:, g * GROUP_DIM:(g + 1) * GROUP_DIM]
        normed.append(part * lax.rsqrt(jnp.mean(part * part, axis=-1, keepdims=True) + EPS))
    ssd_out = jnp.concatenate(normed, axis=1) * ng_ref[...]
    o_ref[:, A_DIM:A_DIM + SSD_DIM] = ssd_out.astype(BF16)

    v = _gelu(v_ref[...])
    vc = v - jnp.mean(v, axis=-1, keepdims=True)
    vn = vc * lax.rsqrt(jnp.mean(vc * vc, axis=-1, keepdims=True) + EPS) * lg_ref[...] + lb_ref[...]
    vn = vn.astype(BF16)
    for h in range(A_HEADS):
        hs = slice(h * A_HEAD_DIM, (h + 1) * A_HEAD_DIM)
        s = _dot(ws_ref[h], vn[:, hs]) + bs_ref[:, hs]
        o_ref[:, hs] = (_gelu(u_ref[:, hs]) * s).astype(BF16)


def _mixer(p, dt, prev_f, prev_b, consts, extra, *, nb, nc):
    cur, prev, nxt, dts = _halo_specs(nc, (2 * A_DIM + SSD_DIM) // XBC_DIM, False)
    lane_blk = lambda k: pl.BlockSpec((CHUNK, A_DIM), lambda b, c: (b * nc + c, k))
    state_spec = pl.BlockSpec((None, SSD_STATE, SSD_DIM), lambda b, c: (b * nc + c, 0, 0))
    params = tuple(consts) + tuple(extra)
    args = (p, p, p, p, p, p, dt, prev_f, prev_b) + params
    in_specs = [lane_blk(0), lane_blk(1), lane_blk(2), cur, prev, nxt, dts, state_spec, state_spec]
    in_specs += [_full_spec(a) for a in params]
    return pl.pallas_call(
        functools.partial(_mixer_kernel, nc=nc),
        grid=(nb, nc),
        in_specs=in_specs,
        out_specs=pl.BlockSpec((CHUNK, A_DIM + SSD_DIM), lambda b, c: (b * nc + c, 0)),
        out_shape=jax.ShapeDtypeStruct((nb * nc * CHUNK, A_DIM + SSD_DIM), BF16),
        scratch_shapes=[pltpu.VMEM((PAD_ROWS, XBC_DIM), F32)],
        compiler_params=_params(("arbitrary", "arbitrary")),
        name="mixer",
    )(*args)


def _outproj_kernel(mix_ref, x_ref, gate_ref, w_ref, o_ref):
    o_ref[...] = x_ref[...] + gate_ref[...] * _dot(mix_ref[...], w_ref[...])


def _outproj(mix, x2d, gate, w, *, rows_per_mod, tm=512):
    t = x2d.shape[0]
    tiles_per_mod = rows_per_mod // tm
    tile = pl.BlockSpec((tm, D_MODEL), lambda i: (i, 0))
    return pl.pallas_call(
        _outproj_kernel,
        grid=(t // tm,),
        in_specs=[tile, tile,
                  pl.BlockSpec((None, 1, D_MODEL), lambda i: (i // tiles_per_mod, 0, 0)),
                  pl.BlockSpec((D_MODEL, D_MODEL), lambda i: (0, 0))],
        out_specs=tile,
        out_shape=jax.ShapeDtypeStruct((t, D_MODEL), F32),
        compiler_params=_params(("arbitrary",)),
        name="outproj",
    )(mix, x2d, gate, w)


def _ffn_weights(w_gate, w_up, w_down):
    pad = D_FF_PAD - D_FF
    return (jnp.pad(w_gate, ((0, 0), (0, pad))).astype(BF16),
            jnp.pad(w_up, ((0, 0), (0, pad))).astype(BF16),
            jnp.pad(w_down, ((0, pad), (0, 0))).astype(BF16))


def _ssd_consts(conv_w, conv_b, dt_bias, a_log):
    lanes_pad = LANES - 2 * SSD_HEADS
    bias_row = jnp.pad(dt_bias.reshape(1, 2 * SSD_HEADS), ((0, 0), (0, lanes_pad)))
    a_row = jnp.pad(-jnp.exp(a_log.reshape(1, 2 * SSD_HEADS)), ((0, 0), (0, lanes_pad)))
    tl = jnp.asarray(np.tril(np.ones((CHUNK, CHUNK), np.float32)))
    head_of_lane = np.arange(SSD_DIM) // SSD_HEAD_DIM
    sel = np.arange(LANES)[:, None] == head_of_lane[None, :]
    sel_b = np.arange(LANES)[:, None] == (head_of_lane[None, :] + SSD_HEADS)
    ef = jnp.asarray(sel.astype(np.float32)).astype(BF16)
    eb = jnp.asarray(sel_b.astype(np.float32)).astype(BF16)
    return (conv_w, conv_b.reshape(1, XBC_DIM), bias_row, a_row, tl, tl.T, ef, eb)


def kernel(x, c, ctx, c_ctx, w_mod, b_mod, norm_ffn1, ffn1_w_gate, ffn1_w_up, ffn1_w_down,
           norm_mix, w_in, conv_w, conv_b, dt_bias, a_log, d_skip, ssd_norm_g, gmlp_norm_g,
           gmlp_norm_b, gmlp_w_s, gmlp_b_s, w_out, norm_ffn2, ffn2_w_gate, ffn2_w_up,
           ffn2_w_down, norm_final):
    assert w_mod.shape[0] == 1, "single-layer block"
    bsz, seq, _ = x.shape
    ctx_len = ctx.shape[1]
    nc_x = seq // CHUNK
    nc_c = ctx_len // CHUNK

    pad_rows = (-(bsz + 1)) % SUBLANES
    c_all = jnp.concatenate([c, c_ctx[None, :], jnp.zeros((pad_rows, D_MODEL), F32)], axis=0)
    mod = _modulation(c_all, w_mod[0], b_mod[0]).reshape(-1, N_MOD, D_MODEL)
    mod_x = [mod[:bsz, k][:, None, :] for k in range(N_MOD)]
    mod_c = [mod[bsz:bsz + 1, k][:, None, :] for k in range(N_MOD)]

    row = lambda a: a.reshape(1, -1)
    x2d = x.reshape(bsz * seq, D_MODEL)
    c2d = ctx.reshape(bsz * ctx_len, D_MODEL)

    w1 = _ffn_weights(ffn1_w_gate[0], ffn1_w_up[0], ffn1_w_down[0])
    g1 = row(norm_ffn1[0])
    x1 = _ffn(x2d, mod_x[0], mod_x[1], mod_x[2], g1, *w1, g1, rows_per_mod=seq, final_norm=False)
    c1 = _ffn(c2d, mod_c[0], mod_c[1], mod_c[2], g1, *w1, g1, rows_per_mod=bsz * ctx_len,
              final_norm=False)

    w_main = w_in[0][:, :MAIN_COLS].astype(BF16)
    w_dt = jnp.pad(w_in[0][:, MAIN_COLS:], ((0, 0), (0, LANES - 2 * SSD_HEADS))).astype(BF16)
    gm = row(norm_mix[0])
    px, dtx = _inproj(x1, mod_x[3], mod_x[4], gm, w_main, w_dt, rows_per_mod=seq)
    pc, dtc = _inproj(c1, mod_c[3], mod_c[4], gm, w_main[:, MAIN_COLS - XBC_DIM:], w_dt,
                      rows_per_mod=bsz * ctx_len)

    consts = _ssd_consts(conv_w[0], conv_b[0], dt_bias[0], a_log[0])
    h0 = jnp.zeros((bsz, SSD_STATE, SSD_DIM), F32)
    _, _, hf, hb = _scan(pc, dtc, h0, h0, consts, nb=bsz, nc=nc_c, col=0)
    prev_f, prev_b, _, _ = _scan(px, dtx, hf, hb, consts, nb=bsz, nc=nc_x,
                                 col=(MAIN_COLS - XBC_DIM) // XBC_DIM)

    extra = (row(jnp.repeat(d_skip[0], SSD_HEAD_DIM)), row(ssd_norm_g[0]), row(gmlp_norm_g[0]),
             row(gmlp_norm_b[0]), gmlp_w_s[0].astype(BF16),
             jnp.repeat(gmlp_b_s[0].T, A_HEAD_DIM, axis=1))
    mix = _mixer(px, dtx, prev_f, prev_b, consts, extra, nb=bsz, nc=nc_x)
    x2 = _outproj(mix, x1, mod_x[5], w_out[0].astype(BF16), rows_per_mod=seq)

    w2 = _ffn_weights(ffn2_w_gate[0], ffn2_w_up[0], ffn2_w_down[0])
    out = _ffn(x2, mod_x[6], mod_x[7], mod_x[8], row(norm_ffn2[0]), *w2, row(norm_final),
               rows_per_mod=seq, final_norm=True)
    return out.reshape(bsz, seq, D_MODEL)
```

```python
import functools
import math

import jax
import jax.numpy as jnp
import numpy as np
from jax import lax
from jax.experimental import pallas as pl
from jax.experimental.pallas import tpu as pltpu

F32 = jnp.float32
BF16 = jnp.bfloat16

D_MODEL = 2048
N_MOD = 9
D_FF = 5504
A_HEADS = 8
A_HEAD_DIM = 128
A_DIM = A_HEADS * A_HEAD_DIM
SSD_HEADS = 16
SSD_HEAD_DIM = 64
SSD_DIM = SSD_HEADS * SSD_HEAD_DIM
SSD_GROUPS = 2
SSD_STATE = 128
GROUP_DIM = SSD_DIM // SSD_GROUPS
CHUNK = 128
CONV_W = 5
XBC_DIM = SSD_DIM + 2 * SSD_GROUPS * SSD_STATE
MAIN_COLS = 2 * A_DIM + SSD_DIM + XBC_DIM
EPS = 1e-6

LANES = 128
SUBLANES = 8
VMEM_LIMIT_BYTES = 60 * 1024 * 1024

HALO = SUBLANES
PAD_ROWS = CHUNK + 2 * HALO


def _params(sem):
    return pltpu.CompilerParams(dimension_semantics=sem, vmem_limit_bytes=VMEM_LIMIT_BYTES)


def _dot(a, b):
    return jnp.dot(a, b, preferred_element_type=F32)


def _silu(x):
    return x * jax.nn.sigmoid(x)


def _gelu(x):
    return 0.5 * x * (1.0 + lax.erf(x * (1.0 / math.sqrt(2.0))))


def _softplus(x):
    return jnp.maximum(x, 0.0) + jnp.log1p(jnp.exp(-jnp.abs(x)))


def _rms(x, g):
    return x * lax.rsqrt(jnp.mean(x * x, axis=-1, keepdims=True) + EPS) * g


def _mod_kernel(c_ref, w_ref, b_ref, o_ref):
    o_ref[...] = _dot(_silu(c_ref[...]), w_ref[...]) + b_ref[...]


def _modulation(c_all, w_mod, b_mod, tn=1024):
    rows = c_all.shape[0]
    n = w_mod.shape[1]
    return pl.pallas_call(
        _mod_kernel,
        grid=(n // tn,),
        in_specs=[pl.BlockSpec((rows, D_MODEL), lambda j: (0, 0)),
                  pl.BlockSpec((D_MODEL, tn), lambda j: (0, j)),
                  pl.BlockSpec((1, tn), lambda j: (0, j))],
        out_specs=pl.BlockSpec((rows, tn), lambda j: (0, j)),
        out_shape=jax.ShapeDtypeStruct((rows, n), F32),
        compiler_params=_params(("arbitrary",)),
        name="mod",
    )(c_all, w_mod, b_mod.reshape(1, n))


FFN_TM = 512
FFN_TF = 512
FFN_NORM_ROWS = 64


def _norm_mod(x, g, shift, scale):
    return (_rms(x, g) * (1.0 + scale) + shift).astype(BF16)


def _ffn_kernel(x_ref, xn_ref, shift_ref, scale_ref, gate_ref, shiftn_ref, scalen_ref, g_ref,
                wg_ref, wu_ref, wd_ref, gf_ref, o_ref, h_ref, hn_ref, *, final_norm, last_cols):
    i = pl.program_id(0)
    j = pl.program_id(1)
    last = pl.num_programs(1) - 1

    @pl.when((i == 0) & (j == 0))
    def _():
        h_ref[...] = _norm_mod(x_ref[...], g_ref[...], shift_ref[...], scale_ref[...])

    def partial_ffn(cols):
        blocks = FFN_TM // FFN_NORM_ROWS
        r0 = pl.multiple_of(jnp.minimum(j, blocks - 1) * FFN_NORM_ROWS, FFN_NORM_ROWS)
        rows = pl.ds(r0, FFN_NORM_ROWS)
        hn_ref[rows, :] = _norm_mod(xn_ref[rows, :], g_ref[...], shiftn_ref[...], scalen_ref[...])
        h = h_ref[...]
        gate = _dot(h, wg_ref[:, 0:cols])
        up = _dot(h, wu_ref[:, 0:cols])
        a = (_silu(gate) * up).astype(BF16)
        return _dot(a, wd_ref[0:cols, :])

    @pl.when(j == 0)
    def _():
        o_ref[...] = partial_ffn(FFN_TF)

    @pl.when((j > 0) & (j < last))
    def _():
        o_ref[...] += partial_ffn(FFN_TF)

    @pl.when(j == last)
    def _():
        f = o_ref[...] + partial_ffn(last_cols)
        r = x_ref[...] + (0.5 * gate_ref[...]) * f
        if final_norm:
            r = _rms(r, gf_ref[...])
        o_ref[...] = r
        h_ref[...] = hn_ref[...]


def _ffn(x2d, shift, scale, gate, g, wg, wu, wd, gf, *, rows_per_mod, final_norm):
    t = x2d.shape[0]
    tm, tf = FFN_TM, FFN_TF
    nt = t // tm
    nf = pl.cdiv(D_FF, tf)
    assert nf >= tm // FFN_NORM_ROWS and nf >= 3
    tiles_per_mod = rows_per_mod // tm
    nxt = lambda i: jnp.minimum(i + 1, nt - 1)
    mod_spec = pl.BlockSpec((None, 1, D_MODEL), lambda i, j: (i // tiles_per_mod, 0, 0))
    modn_spec = pl.BlockSpec((None, 1, D_MODEL), lambda i, j: (nxt(i) // tiles_per_mod, 0, 0))
    row_spec = pl.BlockSpec((1, D_MODEL), lambda i, j: (0, 0))
    tile_spec = pl.BlockSpec((tm, D_MODEL), lambda i, j: (i, 0))
    return pl.pallas_call(
        functools.partial(_ffn_kernel, final_norm=final_norm, last_cols=D_FF - (nf - 1) * tf),
        grid=(nt, nf),
        in_specs=[tile_spec, pl.BlockSpec((tm, D_MODEL), lambda i, j: (nxt(i), 0)),
                  mod_spec, mod_spec, mod_spec, modn_spec, modn_spec, row_spec,
                  pl.BlockSpec((D_MODEL, tf), lambda i, j: (0, j)),
                  pl.BlockSpec((D_MODEL, tf), lambda i, j: (0, j)),
                  pl.BlockSpec((tf, D_MODEL), lambda i, j: (j, 0)),
                  row_spec],
        out_specs=tile_spec,
        out_shape=jax.ShapeDtypeStruct((t, D_MODEL), F32),
        scratch_shapes=[pltpu.VMEM((tm, D_MODEL), BF16), pltpu.VMEM((tm, D_MODEL), BF16)],
        compiler_params=_params(("arbitrary", "arbitrary")),
        name="ffn_final" if final_norm else "ffn",
    )(x2d, x2d, shift, scale, gate, shift, scale, g, wg, wu, wd, gf)


def _inproj_kernel(x_ref, shift_ref, scale_ref, g_ref, w_ref, wdt_ref, p_ref, dt_ref, h_ref):
    @pl.when(pl.program_id(1) == 0)
    def _():
        h = _rms(x_ref[...], g_ref[...]) * (1.0 + scale_ref[...]) + shift_ref[...]
        h_ref[...] = h.astype(BF16)
        dt_ref[...] = _dot(h_ref[...], wdt_ref[...])

    p_ref[...] = _dot(h_ref[...], w_ref[...])


def _inproj(x2d, shift, scale, g, w, wdt, *, rows_per_mod, tn=XBC_DIM):
    t = x2d.shape[0]
    n = w.shape[1]
    tm = min(1024, rows_per_mod)
    tiles_per_mod = rows_per_mod // tm
    mod_spec = pl.BlockSpec((None, 1, D_MODEL), lambda i, j: (i // tiles_per_mod, 0, 0))
    return pl.pallas_call(
        _inproj_kernel,
        grid=(t // tm, n // tn),
        in_specs=[pl.BlockSpec((tm, D_MODEL), lambda i, j: (i, 0)), mod_spec, mod_spec,
                  pl.BlockSpec((1, D_MODEL), lambda i, j: (0, 0)),
                  pl.BlockSpec((D_MODEL, tn), lambda i, j: (0, j)),
                  pl.BlockSpec((D_MODEL, LANES), lambda i, j: (0, 0))],
        out_specs=[pl.BlockSpec((tm, tn), lambda i, j: (i, j)),
                   pl.BlockSpec((tm, LANES), lambda i, j: (i, 0))],
        out_shape=[jax.ShapeDtypeStruct((t, n), F32), jax.ShapeDtypeStruct((t, LANES), F32)],
        scratch_shapes=[pltpu.VMEM((tm, D_MODEL), BF16)],
        compiler_params=_params(("arbitrary", "arbitrary")),
        name="inproj",
    )(x2d, shift, scale, g, w, wdt)


def _conv_silu(cur_ref, prev_ref, next_ref, w_ref, b_ref, pad_ref, is_first, is_last, ncols):
    outs = []
    for s in range(ncols // LANES):
        ls = slice(s * LANES, (s + 1) * LANES)
        pad_ref[s, 0:HALO, :] = jnp.where(is_first, 0.0, prev_ref[:, ls])
        pad_ref[s, HALO:HALO + CHUNK, :] = cur_ref[:, ls]
        pad_ref[s, HALO + CHUNK:PAD_ROWS, :] = jnp.where(is_last, 0.0, next_ref[:, ls])
        acc = jnp.broadcast_to(b_ref[:, ls], (CHUNK, LANES))
        for k in range(CONV_W):
            start = HALO - CONV_W // 2 + k
            acc = acc + pad_ref[s, pl.ds(start, CHUNK, stride=1), :] * w_ref[k:k + 1, ls]
        outs.append(_silu(acc))
    return jnp.concatenate(outs, axis=1)


def _dt_terms(dt_raw, bias_row, a_row, tl, tu):
    dt = _softplus(dt_raw + bias_row)
    a = dt * a_row
    s = jnp.dot(tl, a, precision=lax.Precision.HIGHEST, preferred_element_type=F32)
    r = jnp.dot(tu, a, precision=lax.Precision.HIGHEST, preferred_element_type=F32)
    is_fwd = lax.broadcasted_iota(jnp.int32, (CHUNK, LANES), 1) < SSD_HEADS
    cum = jnp.where(is_fwd, s, r)
    total = jnp.where(is_fwd[0:1], s[CHUNK - 1:CHUNK], r[0:1])
    return dt, cum, total


def _expand(v, e):
    hi = v.astype(BF16)
    lo = (v - hi.astype(F32)).astype(BF16)
    return _dot(hi, e) + _dot(lo, e)


def _scan_kernel(xc_ref, xp_ref, xn_ref, dt_ref, h0f_ref, h0b_ref, cw_ref, cb_ref, bias_ref,
                 arow_ref, tl_ref, tu_ref, ef_ref, eb_ref, act_ref, pf_ref, pb_ref, ff_ref, fb_ref,
                 pad_ref, sf_ref, sball_ref, dball_ref, *, nc):
    c = pl.program_id(1)

    @pl.when(c == 0)
    def _():
        sf_ref[...] = h0f_ref[...]

    xb = _conv_silu(xc_ref, xp_ref, xn_ref, cw_ref, cb_ref, pad_ref, c == 0, c == nc - 1, XBC_DIM)
    act_ref[...] = xb
    dt, cum, total = _dt_terms(dt_ref[...], bias_ref[...], arow_ref[...], tl_ref[...], tu_ref[...])
    w = dt * jnp.exp(total - cum)
    decay = jnp.broadcast_to(jnp.exp(total), (SUBLANES, LANES))
    v = jnp.concatenate([w, decay], axis=0)
    xs = xb[:, 0:SSD_DIM]
    bt = []
    for g in range(SSD_GROUPS):
        lo = SSD_DIM + g * SSD_STATE
        bt.append(xb[:, lo:lo + SSD_STATE].T.astype(BF16))

    def chunk_state(e_ref):
        ex = _expand(v, e_ref[...])
        xw = (xs * ex[0:CHUNK]).astype(BF16)
        new = [_dot(bt[g], xw[:, g * GROUP_DIM:(g + 1) * GROUP_DIM]) for g in range(SSD_GROUPS)]
        return jnp.concatenate(new, axis=1), ex[CHUNK:CHUNK + SUBLANES]

    new_f, decay_f = chunk_state(ef_ref)
    state = sf_ref[...]
    pf_ref[...] = state.astype(BF16)
    sf_ref[...] = state * decay_f[0:1] + new_f

    new_b, decay_b = chunk_state(eb_ref)
    sball_ref[c] = new_b
    dball_ref[c] = decay_b

    @pl.when(c == nc - 1)
    def _():
        ff_ref[...] = sf_ref[...]
        sb = h0b_ref[...]
        for cc in reversed(range(nc)):
            pb_ref[cc] = sb.astype(BF16)
            sb = sb * dball_ref[cc, 0:1, :] + sball_ref[cc]
        fb_ref[...] = sb


def _halo_specs(nc, col):
    per = CHUNK // HALO
    cur = pl.BlockSpec((CHUNK, XBC_DIM), lambda b, c: (b * nc + c, col))
    prev = pl.BlockSpec((HALO, XBC_DIM), lambda b, c: (jnp.maximum((b * nc + c) * per - 1, 0), col))
    nxt = pl.BlockSpec((HALO, XBC_DIM),
                       lambda b, c: (jnp.minimum((b * nc + c + 1) * per, (b * nc + nc) * per - 1), col))
    return cur, prev, nxt


def _full_spec(a):
    return pl.BlockSpec(a.shape, lambda b, c: (0,) * a.ndim)


def _scan(p, dt, h0f, h0b, consts, *, nb, nc, col):
    state_spec = pl.BlockSpec((None, SSD_STATE, SSD_DIM), lambda b, c: (b, 0, 0))
    chunk_rows = lambda width: pl.BlockSpec((CHUNK, width), lambda b, c: (b * nc + c, 0))
    args = (p, p, p, dt, h0f, h0b) + tuple(consts)
    in_specs = list(_halo_specs(nc, col)) + [chunk_rows(LANES), state_spec, state_spec]
    in_specs += [_full_spec(a) for a in consts]
    return pl.pallas_call(
        functools.partial(_scan_kernel, nc=nc),
        grid=(nb, nc),
        in_specs=in_specs,
        out_specs=[chunk_rows(XBC_DIM),
                   pl.BlockSpec((None, SSD_STATE, SSD_DIM), lambda b, c: (b * nc + c, 0, 0)),
                   pl.BlockSpec((nc, SSD_STATE, SSD_DIM), lambda b, c: (b, 0, 0)),
                   state_spec, state_spec],
        out_shape=[jax.ShapeDtypeStruct((nb * nc * CHUNK, XBC_DIM), F32),
                   jax.ShapeDtypeStruct((nb * nc, SSD_STATE, SSD_DIM), BF16),
                   jax.ShapeDtypeStruct((nb * nc, SSD_STATE, SSD_DIM), BF16),
                   jax.ShapeDtypeStruct((nb, SSD_STATE, SSD_DIM), F32),
                   jax.ShapeDtypeStruct((nb, SSD_STATE, SSD_DIM), F32)],
        scratch_shapes=[pltpu.VMEM((XBC_DIM // LANES, PAD_ROWS, LANES), F32),
                        pltpu.VMEM((SSD_STATE, SSD_DIM), F32),
                        pltpu.VMEM((nc, SSD_STATE, SSD_DIM), F32),
                        pltpu.VMEM((nc, SUBLANES, SSD_DIM), F32)],
        compiler_params=_params(("arbitrary", "arbitrary")),
        name="ssd_scan",
    )(*args)


def _mixer_kernel(u_ref, v_ref, z_ref, act_ref, dt_ref, pf_ref, pb_ref,
                  bias_ref, arow_ref, tl_ref, tu_ref, ef_ref, eb_ref,
                  dskip_ref, ng_ref, lg_ref, lb_ref, ws_ref, bs_ref, o_ref):
    xs = act_ref[:, 0:SSD_DIM]
    gn = SSD_GROUPS * SSD_STATE
    bm = act_ref[:, SSD_DIM:SSD_DIM + gn].astype(BF16)
    cm = act_ref[:, SSD_DIM + gn:XBC_DIM].astype(BF16)

    dt, cum, _ = _dt_terms(dt_ref[...], bias_ref[...], arow_ref[...], tl_ref[...], tu_ref[...])
    ecum = jnp.exp(cum)
    ecum_f = _expand(ecum, ef_ref[...])
    ecum_b = _expand(ecum, eb_ref[...])
    dt_t = dt.T
    cum_t = cum.T

    lane = lax.broadcasted_iota(jnp.int32, (CHUNK, SSD_DIM), 1)
    first_head = (lane % LANES) < SSD_HEAD_DIM
    xs_even = jnp.where(first_head, xs, 0.0).astype(BF16)
    xs_odd = jnp.where(first_head, 0.0, xs).astype(BF16)

    row = lax.broadcasted_iota(jnp.int32, (CHUNK, CHUNK), 0)
    col = lax.broadcasted_iota(jnp.int32, (CHUNK, CHUNK), 1)
    causal = row >= col
    anticausal = row <= col

    def decay_matrix(h):
        lf = jnp.exp(jnp.minimum(cum[:, h:h + 1] - cum_t[h:h + 1, :], 0.0))
        lf = jnp.where(causal, lf, 0.0) * dt_t[h:h + 1, :]
        hb = SSD_HEADS + h
        lb = jnp.exp(jnp.minimum(cum[:, hb:hb + 1] - cum_t[hb:hb + 1, :], 0.0))
        lb = jnp.where(anticausal, lb, 0.0) * dt_t[hb:hb + 1, :]
        return lf + lb

    pf = pf_ref[...]
    pb = pb_ref[...]
    heads_per_group = SSD_HEADS // SSD_GROUPS
    y_parts = []
    for g in range(SSD_GROUPS):
        cg = cm[:, g * SSD_STATE:(g + 1) * SSD_STATE]
        bg = bm[:, g * SSD_STATE:(g + 1) * SSD_STATE]
        cb_g = lax.dot_general(cg, bg, (((1,), (1,)), ((), ())), preferred_element_type=F32)
        gs = slice(g * GROUP_DIM, (g + 1) * GROUP_DIM)
        y_off = ecum_f[:, gs] * _dot(cg, pf[:, gs]) + ecum_b[:, gs] * _dot(cg, pb[:, gs])
        diag = []
        for j in range(heads_per_group // 2):
            h0 = g * heads_per_group + 2 * j
            m_pair = jnp.concatenate([(cb_g * decay_matrix(h0)).astype(BF16),
                                      (cb_g * decay_matrix(h0 + 1)).astype(BF16)], axis=1)
            ls = slice(h0 * SSD_HEAD_DIM, (h0 + 2) * SSD_HEAD_DIM)
            rhs = jnp.concatenate([xs_even[:, ls], xs_odd[:, ls]], axis=0)
            diag.append(_dot(m_pair, rhs))
        y_parts.append(jnp.concatenate(diag, axis=1) + y_off)
    y = jnp.concatenate(y_parts, axis=1) + dskip_ref[...] * xs

    yz = y * _silu(z_ref[...])
    normed = []
    for g in range(SSD_GROUPS):
        part = yz[:, g * GROUP_DIM:(g + 1) * GROUP_DIM]
        normed.append(part * lax.rsqrt(jnp.mean(part * part, axis=-1, keepdims=True) + EPS))
    ssd_out = jnp.concatenate(normed, axis=1) * ng_ref[...]
    o_ref[:, A_DIM:A_DIM + SSD_DIM] = ssd_out.astype(BF16)

    v = _gelu(v_ref[...])
    vc = v - jnp.mean(v, axis=-1, keepdims=True)
    vn = vc * lax.rsqrt(jnp.mean(vc * vc, axis=-1, keepdims=True) + EPS) * lg_ref[...] + lb_ref[...]
    vn = vn.astype(BF16)
    for h in range(A_HEADS):
        hs = slice(h * A_HEAD_DIM, (h + 1) * A_HEAD_DIM)
        s = _dot(ws_ref[h], vn[:, hs]) + bs_ref[:, hs]
        o_ref[:, hs] = (_gelu(u_ref[:, hs]) * s).astype(BF16)


def _mixer(p, act, dt, prev_f, prev_b, consts, extra, *, nb, nc):
    lane_blk = lambda k: pl.BlockSpec((CHUNK, A_DIM), lambda b, c: (b * nc + c, k))
    chunk_rows = lambda width: pl.BlockSpec((CHUNK, width), lambda b, c: (b * nc + c, 0))
    state_spec = pl.BlockSpec((None, SSD_STATE, SSD_DIM), lambda b, c: (b * nc + c, 0, 0))
    params = tuple(consts[2:]) + tuple(extra)
    args = (p, p, p, act, dt, prev_f, prev_b) + params
    in_specs = [lane_blk(0), lane_blk(1), lane_blk(2), chunk_rows(XBC_DIM), chunk_rows(LANES),
                state_spec, state_spec]
    in_specs += [_full_spec(a) for a in params]
    return pl.pallas_call(
        _mixer_kernel,
        grid=(nb, nc),
        in_specs=in_specs,
        out_specs=pl.BlockSpec((CHUNK, A_DIM + SSD_DIM), lambda b, c: (b * nc + c, 0)),
        out_shape=jax.ShapeDtypeStruct((nb * nc * CHUNK, A_DIM + SSD_DIM), BF16),
        compiler_params=_params(("arbitrary", "arbitrary")),
        name="mixer",
    )(*args)


def _outproj_kernel(mix_ref, x_ref, gate_ref, w_ref, o_ref):
    o_ref[...] = x_ref[...] + gate_ref[...] * _dot(mix_ref[...], w_ref[...])


def _outproj(mix, x2d, gate, w, *, rows_per_mod, tm=512):
    t = x2d.shape[0]
    tiles_per_mod = rows_per_mod // tm
    tile = pl.BlockSpec((tm, D_MODEL), lambda i: (i, 0))
    return pl.pallas_call(
        _outproj_kernel,
        grid=(t // tm,),
        in_specs=[tile, tile,
                  pl.BlockSpec((None, 1, D_MODEL), lambda i: (i // tiles_per_mod, 0, 0)),
                  pl.BlockSpec((D_MODEL, D_MODEL), lambda i: (0, 0))],
        out_specs=tile,
        out_shape=jax.ShapeDtypeStruct((t, D_MODEL), F32),
        compiler_params=_params(("arbitrary",)),
        name="outproj",
    )(mix, x2d, gate, w)


def _ffn_weights(w_gate, w_up, w_down):
    return w_gate.astype(BF16), w_up.astype(BF16), w_down.astype(BF16)


def _ssd_consts(conv_w, conv_b, dt_bias, a_log):
    lanes_pad = LANES - 2 * SSD_HEADS
    bias_row = jnp.pad(dt_bias.reshape(1, 2 * SSD_HEADS), ((0, 0), (0, lanes_pad)))
    a_row = jnp.pad(-jnp.exp(a_log.reshape(1, 2 * SSD_HEADS)), ((0, 0), (0, lanes_pad)))
    tl = jnp.asarray(np.tril(np.ones((CHUNK, CHUNK), np.float32)))
    head_of_lane = np.arange(SSD_DIM) // SSD_HEAD_DIM
    sel = np.arange(LANES)[:, None] == head_of_lane[None, :]
    sel_b = np.arange(LANES)[:, None] == (head_of_lane[None, :] + SSD_HEADS)
    ef = jnp.asarray(sel.astype(np.float32)).astype(BF16)
    eb = jnp.asarray(sel_b.astype(np.float32)).astype(BF16)
    return (conv_w, conv_b.reshape(1, XBC_DIM), bias_row, a_row, tl, tl.T, ef, eb)


def kernel(x, c, ctx, c_ctx, w_mod, b_mod, norm_ffn1, ffn1_w_gate, ffn1_w_up, ffn1_w_down,
           norm_mix, w_in, conv_w, conv_b, dt_bias, a_log, d_skip, ssd_norm_g, gmlp_norm_g,
           gmlp_norm_b, gmlp_w_s, gmlp_b_s, w_out, norm_ffn2, ffn2_w_gate, ffn2_w_up,
           ffn2_w_down, norm_final):
    assert w_mod.shape[0] == 1, "single-layer block"
    bsz, seq, _ = x.shape
    ctx_len = ctx.shape[1]
    nc_x = seq // CHUNK
    nc_c = ctx_len // CHUNK

    pad_rows = (-(bsz + 1)) % SUBLANES
    c_all = jnp.concatenate([c, c_ctx[None, :], jnp.zeros((pad_rows, D_MODEL), F32)], axis=0)
    mod = _modulation(c_all, w_mod[0], b_mod[0]).reshape(-1, N_MOD, D_MODEL)
    mod_x = [mod[:bsz, k][:, None, :] for k in range(N_MOD)]
    mod_c = [mod[bsz:bsz + 1, k][:, None, :] for k in range(N_MOD)]

    row = lambda a: a.reshape(1, -1)
    x2d = x.reshape(bsz * seq, D_MODEL)
    c2d = ctx.reshape(bsz * ctx_len, D_MODEL)

    w1 = _ffn_weights(ffn1_w_gate[0], ffn1_w_up[0], ffn1_w_down[0])
    g1 = row(norm_ffn1[0])
    x1 = _ffn(x2d, mod_x[0], mod_x[1], mod_x[2], g1, *w1, g1, rows_per_mod=seq, final_norm=False)
    c1 = _ffn(c2d, mod_c[0], mod_c[1], mod_c[2], g1, *w1, g1, rows_per_mod=bsz * ctx_len,
              final_norm=False)

    w_main = w_in[0][:, :MAIN_COLS].astype(BF16)
    w_dt = jnp.pad(w_in[0][:, MAIN_COLS:], ((0, 0), (0, LANES - 2 * SSD_HEADS))).astype(BF16)
    gm = row(norm_mix[0])
    px, dtx = _inproj(x1, mod_x[3], mod_x[4], gm, w_main, w_dt, rows_per_mod=seq)
    pc, dtc = _inproj(c1, mod_c[3], mod_c[4], gm, w_main[:, MAIN_COLS - XBC_DIM:], w_dt,
                      rows_per_mod=bsz * ctx_len)

    consts = _ssd_consts(conv_w[0], conv_b[0], dt_bias[0], a_log[0])
    h0 = jnp.zeros((bsz, SSD_STATE, SSD_DIM), F32)
    _, _, _, hf, hb = _scan(pc, dtc, h0, h0, consts, nb=bsz, nc=nc_c, col=0)
    act, prev_f, prev_b, _, _ = _scan(px, dtx, hf, hb, consts, nb=bsz, nc=nc_x,
                                      col=(MAIN_COLS - XBC_DIM) // XBC_DIM)

    extra = (row(jnp.repeat(d_skip[0], SSD_HEAD_DIM)), row(ssd_norm_g[0]), row(gmlp_norm_g[0]),
             row(gmlp_norm_b[0]), gmlp_w_s[0].astype(BF16),
             jnp.repeat(gmlp_b_s[0].T, A_HEAD_DIM, axis=1))
    mix = _mixer(px, act, dtx, prev_f, prev_b, consts, extra, nb=bsz, nc=nc_x)
    x2 = _outproj(mix, x1, mod_x[5], w_out[0].astype(BF16), rows_per_mod=seq)

    w2 = _ffn_weights(ffn2_w_gate[0], ffn2_w_up[0], ffn2_w_down[0])
    out = _ffn(x2, mod_x[6], mod_x[7], mod_x[8], row(norm_ffn2[0]), *w2, row(norm_final),
               rows_per_mod=seq, final_norm=True)
    return out.reshape(bsz, seq, D_MODEL)
```

```python
import functools
import math

import jax
import jax.numpy as jnp
import numpy as np
from jax import lax
from jax.experimental import pallas as pl
from jax.experimental.pallas import tpu as pltpu

F32 = jnp.float32
BF16 = jnp.bfloat16

D_MODEL = 2048
N_MOD = 9
D_FF = 5504
A_HEADS = 8
A_HEAD_DIM = 128
A_DIM = A_HEADS * A_HEAD_DIM
SSD_HEADS = 16
SSD_HEAD_DIM = 64
SSD_DIM = SSD_HEADS * SSD_HEAD_DIM
SSD_GROUPS = 2
SSD_STATE = 128
GROUP_DIM = SSD_DIM // SSD_GROUPS
CHUNK = 128
CONV_W = 5
XBC_DIM = SSD_DIM + 2 * SSD_GROUPS * SSD_STATE
MAIN_COLS = 2 * A_DIM + SSD_DIM + XBC_DIM
EPS = 1e-6

LANES = 128
SUBLANES = 8
VMEM_LIMIT_BYTES = 60 * 1024 * 1024

HALO = SUBLANES
PAD_ROWS = CHUNK + 2 * HALO


def _params(sem):
    return pltpu.CompilerParams(dimension_semantics=sem, vmem_limit_bytes=VMEM_LIMIT_BYTES)


def _dot(a, b):
    return jnp.dot(a, b, preferred_element_type=F32)


def _silu(x):
    return x * jax.nn.sigmoid(x)


def _gelu(x):
    return 0.5 * x * (1.0 + lax.erf(x * (1.0 / math.sqrt(2.0))))


def _softplus(x):
    return jnp.maximum(x, 0.0) + jnp.log1p(jnp.exp(-jnp.abs(x)))


def _rms(x, g):
    return x * lax.rsqrt(jnp.mean(x * x, axis=-1, keepdims=True) + EPS) * g


def _mod_kernel(c_ref, w_ref, b_ref, o_ref):
    o_ref[...] = _dot(_silu(c_ref[...]), w_ref[...]) + b_ref[...]


def _modulation(c_all, w_mod, b_mod, tn=1024):
    rows = c_all.shape[0]
    n = w_mod.shape[1]
    return pl.pallas_call(
        _mod_kernel,
        grid=(n // tn,),
        in_specs=[pl.BlockSpec((rows, D_MODEL), lambda j: (0, 0)),
                  pl.BlockSpec((D_MODEL, tn), lambda j: (0, j)),
                  pl.BlockSpec((1, tn), lambda j: (0, j))],
        out_specs=pl.BlockSpec((rows, tn), lambda j: (0, j)),
        out_shape=jax.ShapeDtypeStruct((rows, n), F32),
        compiler_params=_params(("arbitrary",)),
        name="mod",
    )(c_all, w_mod, b_mod.reshape(1, n))


FFN_TM = 512
FFN_TF = 1024


def _norm_mod(x, g, shift, scale):
    return (_rms(x, g) * (1.0 + scale) + shift).astype(BF16)


def _ffn_kernel(*refs, final_norm, last_cols, pre_normed):
    if pre_normed:
        x_ref, h_ref, gate_ref, wg_ref, wu_ref, wd_ref, gf_ref, o_ref = refs
    else:
        (x_ref, shift_ref, scale_ref, gate_ref, g_ref, wg_ref, wu_ref, wd_ref, gf_ref,
         o_ref, h_ref) = refs
    j = pl.program_id(1)
    last = pl.num_programs(1) - 1

    if not pre_normed:
        @pl.when(j == 0)
        def _():
            h_ref[...] = _norm_mod(x_ref[...], g_ref[...], shift_ref[...], scale_ref[...])

    def partial_ffn(cols):
        h = h_ref[...]
        gate = _dot(h, wg_ref[:, 0:cols])
        up = _dot(h, wu_ref[:, 0:cols])
        a = (_silu(gate) * up).astype(BF16)
        return _dot(a, wd_ref[0:cols, :])

    @pl.when(j == 0)
    def _():
        o_ref[...] = partial_ffn(FFN_TF)

    @pl.when((j > 0) & (j < last))
    def _():
        o_ref[...] += partial_ffn(FFN_TF)

    @pl.when(j == last)
    def _():
        f = o_ref[...] + partial_ffn(last_cols)
        r = x_ref[...] + (0.5 * gate_ref[...]) * f
        if final_norm:
            r = _rms(r, gf_ref[...])
        o_ref[...] = r


def _ffn(x2d, h2d, shift, scale, gate, g, wg, wu, wd, gf, *, rows_per_mod, final_norm):
    t = x2d.shape[0]
    tm, tf = FFN_TM, FFN_TF
    nf = pl.cdiv(D_FF, tf)
    assert nf >= 3
    pre_normed = h2d is not None
    tiles_per_mod = rows_per_mod // tm
    mod_spec = pl.BlockSpec((None, 1, D_MODEL), lambda i, j: (i // tiles_per_mod, 0, 0))
    row_spec = pl.BlockSpec((1, D_MODEL), lambda i, j: (0, 0))
    tile_spec = pl.BlockSpec((tm, D_MODEL), lambda i, j: (i, 0))
    w_specs = [pl.BlockSpec((D_MODEL, tf), lambda i, j: (0, j)),
               pl.BlockSpec((D_MODEL, tf), lambda i, j: (0, j)),
               pl.BlockSpec((tf, D_MODEL), lambda i, j: (j, 0))]
    if pre_normed:
        args = (x2d, h2d, gate, wg, wu, wd, gf)
        in_specs = [tile_spec, tile_spec, mod_spec] + w_specs + [row_spec]
        scratch = []
    else:
        args = (x2d, shift, scale, gate, g, wg, wu, wd, gf)
        in_specs = [tile_spec, mod_spec, mod_spec, mod_spec, row_spec] + w_specs + [row_spec]
        scratch = [pltpu.VMEM((tm, D_MODEL), BF16)]
    return pl.pallas_call(
        functools.partial(_ffn_kernel, final_norm=final_norm, last_cols=D_FF - (nf - 1) * tf,
                          pre_normed=pre_normed),
        grid=(t // tm, nf),
        in_specs=in_specs,
        out_specs=tile_spec,
        out_shape=jax.ShapeDtypeStruct((t, D_MODEL), F32),
        scratch_shapes=scratch,
        compiler_params=_params(("arbitrary", "arbitrary")),
        name="ffn_final" if final_norm else "ffn",
    )(*args)


def _inproj_kernel(x_ref, shift_ref, scale_ref, g_ref, w_ref, wdt_ref, p_ref, dt_ref, h_ref):
    @pl.when(pl.program_id(1) == 0)
    def _():
        h = _rms(x_ref[...], g_ref[...]) * (1.0 + scale_ref[...]) + shift_ref[...]
        h_ref[...] = h.astype(BF16)
        dt_ref[...] = _dot(h_ref[...], wdt_ref[...])

    p_ref[...] = _dot(h_ref[...], w_ref[...])


def _inproj(x2d, shift, scale, g, w, wdt, *, rows_per_mod, tn=XBC_DIM):
    t = x2d.shape[0]
    n = w.shape[1]
    tm = min(1024, rows_per_mod)
    tiles_per_mod = rows_per_mod // tm
    mod_spec = pl.BlockSpec((None, 1, D_MODEL), lambda i, j: (i // tiles_per_mod, 0, 0))
    return pl.pallas_call(
        _inproj_kernel,
        grid=(t // tm, n // tn),
        in_specs=[pl.BlockSpec((tm, D_MODEL), lambda i, j: (i, 0)), mod_spec, mod_spec,
                  pl.BlockSpec((1, D_MODEL), lambda i, j: (0, 0)),
                  pl.BlockSpec((D_MODEL, tn), lambda i, j: (0, j)),
                  pl.BlockSpec((D_MODEL, LANES), lambda i, j: (0, 0))],
        out_specs=[pl.BlockSpec((tm, tn), lambda i, j: (i, j)),
                   pl.BlockSpec((tm, LANES), lambda i, j: (i, 0))],
        out_shape=[jax.ShapeDtypeStruct((t, n), F32), jax.ShapeDtypeStruct((t, LANES), F32)],
        scratch_shapes=[pltpu.VMEM((tm, D_MODEL), BF16)],
        compiler_params=_params(("arbitrary", "arbitrary")),
        name="inproj",
    )(x2d, shift, scale, g, w, wdt)


def _conv_silu(cur_ref, r0, prev_rows, next_rows, w_ref, b_ref, pad_ref, sub):
    outs = []
    for s in range(XBC_DIM // LANES):
        ls = slice(s * LANES, (s + 1) * LANES)
        pad_ref[sub, s, 0:HALO, :] = prev_rows(ls)
        pad_ref[sub, s, HALO:HALO + CHUNK, :] = cur_ref[r0:r0 + CHUNK, ls]
        pad_ref[sub, s, HALO + CHUNK:PAD_ROWS, :] = next_rows(ls)
        acc = jnp.broadcast_to(b_ref[:, ls], (CHUNK, LANES))
        for k in range(CONV_W):
            start = HALO - CONV_W // 2 + k
            acc = acc + pad_ref[sub, s, pl.ds(start, CHUNK, stride=1), :] * w_ref[k:k + 1, ls]
        outs.append(_silu(acc))
    return jnp.concatenate(outs, axis=1)


def _dt_terms(dt_raw, bias_row, a_row, tl, tu):
    dt = _softplus(dt_raw + bias_row)
    a = dt * a_row
    s = jnp.dot(tl, a, precision=lax.Precision.HIGHEST, preferred_element_type=F32)
    r = jnp.dot(tu, a, precision=lax.Precision.HIGHEST, preferred_element_type=F32)
    is_fwd = lax.broadcasted_iota(jnp.int32, (CHUNK, LANES), 1) < SSD_HEADS
    cum = jnp.where(is_fwd, s, r)
    total = jnp.where(is_fwd[0:1], s[CHUNK - 1:CHUNK], r[0:1])
    return dt, cum, total


def _expand(v, e):
    hi = v.astype(BF16)
    lo = (v - hi.astype(F32)).astype(BF16)
    return _dot(hi, e) + _dot(lo, e)


CHUNKS_PER_STEP = 2
STEP_ROWS = CHUNKS_PER_STEP * CHUNK


def _scan_kernel(xc_ref, xp_ref, xn_ref, dt_ref, h0f_ref, h0b_ref, cw_ref, cb_ref, bias_ref,
                 arow_ref, tl_ref, tu_ref, ef_ref, eb_ref, act_ref, pf_ref, pb_ref, ff_ref, fb_ref,
                 pad_ref, sf_ref, sball_ref, dball_ref, *, nc):
    c = pl.program_id(1)
    nsteps = nc // CHUNKS_PER_STEP

    @pl.when(c == 0)
    def _():
        sf_ref[...] = h0f_ref[...]

    state = sf_ref[...]
    for sub in range(CHUNKS_PER_STEP):
        r0 = sub * CHUNK
        if sub == 0:
            prev_rows = lambda ls: jnp.where(c == 0, 0.0, xp_ref[:, ls])
        else:
            prev_rows = lambda ls, r0=r0: xc_ref[r0 - HALO:r0, ls]
        if sub == CHUNKS_PER_STEP - 1:
            next_rows = lambda ls: jnp.where(c == nsteps - 1, 0.0, xn_ref[:, ls])
        else:
            next_rows = lambda ls, r0=r0: xc_ref[r0 + CHUNK:r0 + CHUNK + HALO, ls]
        xb = _conv_silu(xc_ref, r0, prev_rows, next_rows, cw_ref, cb_ref, pad_ref, sub)
        act_ref[r0:r0 + CHUNK, :] = xb
        dt, cum, total = _dt_terms(dt_ref[r0:r0 + CHUNK, :], bias_ref[...], arow_ref[...],
                                   tl_ref[...], tu_ref[...])
        w = dt * jnp.exp(total - cum)
        decay = jnp.broadcast_to(jnp.exp(total), (SUBLANES, LANES))
        v = jnp.concatenate([w, decay], axis=0)
        xs = xb[:, 0:SSD_DIM]
        bt = []
        for g in range(SSD_GROUPS):
            lo = SSD_DIM + g * SSD_STATE
            bt.append(xb[:, lo:lo + SSD_STATE].T.astype(BF16))

        def chunk_state(e_ref):
            ex = _expand(v, e_ref[...])
            xw = (xs * ex[0:CHUNK]).astype(BF16)
            new = [_dot(bt[g], xw[:, g * GROUP_DIM:(g + 1) * GROUP_DIM])
                   for g in range(SSD_GROUPS)]
            return jnp.concatenate(new, axis=1), ex[CHUNK:CHUNK + SUBLANES]

        new_f, decay_f = chunk_state(ef_ref)
        pf_ref[sub] = state.astype(BF16)
        state = state * decay_f[0:1] + new_f

        new_b, decay_b = chunk_state(eb_ref)
        sball_ref[c * CHUNKS_PER_STEP + sub] = new_b
        dball_ref[c * CHUNKS_PER_STEP + sub] = decay_b
    sf_ref[...] = state

    @pl.when(c == nsteps - 1)
    def _():
        ff_ref[...] = sf_ref[...]
        sb = h0b_ref[...]
        for cc in reversed(range(nc)):
            pb_ref[cc] = sb.astype(BF16)
            sb = sb * dball_ref[cc, 0:1, :] + sball_ref[cc]
        fb_ref[...] = sb


def _halo_specs(nsteps, col):
    per = STEP_ROWS // HALO
    cur = pl.BlockSpec((STEP_ROWS, XBC_DIM), lambda b, c: (b * nsteps + c, col))
    prev = pl.BlockSpec((HALO, XBC_DIM),
                        lambda b, c: (jnp.maximum((b * nsteps + c) * per - 1, 0), col))
    nxt = pl.BlockSpec((HALO, XBC_DIM),
                       lambda b, c: (jnp.minimum((b * nsteps + c + 1) * per,
                                                 (b * nsteps + nsteps) * per - 1), col))
    return cur, prev, nxt


def _full_spec(a):
    return pl.BlockSpec(a.shape, lambda b, c: (0,) * a.ndim)


def _scan(p, dt, h0f, h0b, consts, *, nb, nc, col):
    assert nc % CHUNKS_PER_STEP == 0
    nsteps = nc // CHUNKS_PER_STEP
    state_spec = pl.BlockSpec((None, SSD_STATE, SSD_DIM), lambda b, c: (b, 0, 0))
    step_rows = lambda width: pl.BlockSpec((STEP_ROWS, width), lambda b, c: (b * nsteps + c, 0))
    args = (p, p, p, dt, h0f, h0b) + tuple(consts)
    in_specs = list(_halo_specs(nsteps, col)) + [step_rows(LANES), state_spec, state_spec]
    in_specs += [_full_spec(a) for a in consts]
    return pl.pallas_call(
        functools.partial(_scan_kernel, nc=nc),
        grid=(nb, nsteps),
        in_specs=in_specs,
        out_specs=[step_rows(XBC_DIM),
                   pl.BlockSpec((CHUNKS_PER_STEP, SSD_STATE, SSD_DIM),
                                lambda b, c: (b * nsteps + c, 0, 0)),
                   pl.BlockSpec((nc, SSD_STATE, SSD_DIM), lambda b, c: (b, 0, 0)),
                   state_spec, state_spec],
        out_shape=[jax.ShapeDtypeStruct((nb * nc * CHUNK, XBC_DIM), F32),
                   jax.ShapeDtypeStruct((nb * nc, SSD_STATE, SSD_DIM), BF16),
                   jax.ShapeDtypeStruct((nb * nc, SSD_STATE, SSD_DIM), BF16),
                   jax.ShapeDtypeStruct((nb, SSD_STATE, SSD_DIM), F32),
                   jax.ShapeDtypeStruct((nb, SSD_STATE, SSD_DIM), F32)],
        scratch_shapes=[pltpu.VMEM((CHUNKS_PER_STEP, XBC_DIM // LANES, PAD_ROWS, LANES), F32),
                        pltpu.VMEM((SSD_STATE, SSD_DIM), F32),
                        pltpu.VMEM((nc, SSD_STATE, SSD_DIM), F32),
                        pltpu.VMEM((nc, SUBLANES, SSD_DIM), F32)],
        compiler_params=_params(("arbitrary", "arbitrary")),
        name="ssd_scan",
    )(*args)


def _mixer_kernel(u_ref, v_ref, z_ref, act_ref, dt_ref, pf_ref, pb_ref,
                  bias_ref, arow_ref, tl_ref, tu_ref, ef_ref, eb_ref,
                  dskip_ref, ng_ref, lg_ref, lb_ref, ws_ref, bs_ref, o_ref):
    for sub in range(CHUNKS_PER_STEP):
        _mixer_chunk(sub, u_ref, v_ref, z_ref, act_ref, dt_ref, pf_ref, pb_ref, bias_ref, arow_ref,
                     tl_ref, tu_ref, ef_ref, eb_ref, dskip_ref, ng_ref, lg_ref, lb_ref, ws_ref,
                     bs_ref, o_ref)


def _mixer_chunk(sub, u_ref, v_ref, z_ref, act_ref, dt_ref, pf_ref, pb_ref, bias_ref, arow_ref,
                 tl_ref, tu_ref, ef_ref, eb_ref, dskip_ref, ng_ref, lg_ref, lb_ref, ws_ref, bs_ref,
                 o_ref):
    rows = slice(sub * CHUNK, (sub + 1) * CHUNK)
    xs = act_ref[rows, 0:SSD_DIM]
    gn = SSD_GROUPS * SSD_STATE
    bm = act_ref[rows, SSD_DIM:SSD_DIM + gn].astype(BF16)
    cm = act_ref[rows, SSD_DIM + gn:XBC_DIM].astype(BF16)

    dt, cum, _ = _dt_terms(dt_ref[rows, :], bias_ref[...], arow_ref[...], tl_ref[...], tu_ref[...])
    ecum = jnp.exp(cum)
    ecum_f = _expand(ecum, ef_ref[...])
    ecum_b = _expand(ecum, eb_ref[...])
    dt_t = dt.T
    cum_t = cum.T

    lane = lax.broadcasted_iota(jnp.int32, (CHUNK, SSD_DIM), 1)
    first_head = (lane % LANES) < SSD_HEAD_DIM
    xs_even = jnp.where(first_head, xs, 0.0).astype(BF16)
    xs_odd = jnp.where(first_head, 0.0, xs).astype(BF16)

    row = lax.broadcasted_iota(jnp.int32, (CHUNK, CHUNK), 0)
    col = lax.broadcasted_iota(jnp.int32, (CHUNK, CHUNK), 1)
    causal = row >= col
    anticausal = row <= col

    def decay_matrix(h):
        lf = jnp.exp(jnp.minimum(cum[:, h:h + 1] - cum_t[h:h + 1, :], 0.0))
        lf = jnp.where(causal, lf, 0.0) * dt_t[h:h + 1, :]
        hb = SSD_HEADS + h
        lb = jnp.exp(jnp.minimum(cum[:, hb:hb + 1] - cum_t[hb:hb + 1, :], 0.0))
        lb = jnp.where(anticausal, lb, 0.0) * dt_t[hb:hb + 1, :]
        return lf + lb

    pf = pf_ref[sub]
    pb = pb_ref[sub]
    heads_per_group = SSD_HEADS // SSD_GROUPS
    y_parts = []
    for g in range(SSD_GROUPS):
        cg = cm[:, g * SSD_STATE:(g + 1) * SSD_STATE]
        bg = bm[:, g * SSD_STATE:(g + 1) * SSD_STATE]
        cb_g = lax.dot_general(cg, bg, (((1,), (1,)), ((), ())), preferred_element_type=F32)
        gs = slice(g * GROUP_DIM, (g + 1) * GROUP_DIM)
        y_off = ecum_f[:, gs] * _dot(cg, pf[:, gs]) + ecum_b[:, gs] * _dot(cg, pb[:, gs])
        diag = []
        for j in range(heads_per_group // 2):
            h0 = g * heads_per_group + 2 * j
            m_pair = jnp.concatenate([(cb_g * decay_matrix(h0)).astype(BF16),
                                      (cb_g * decay_matrix(h0 + 1)).astype(BF16)], axis=1)
            ls = slice(h0 * SSD_HEAD_DIM, (h0 + 2) * SSD_HEAD_DIM)
            rhs = jnp.concatenate([xs_even[:, ls], xs_odd[:, ls]], axis=0)
            diag.append(_dot(m_pair, rhs))
        y_parts.append(jnp.concatenate(diag, axis=1) + y_off)
    y = jnp.concatenate(y_parts, axis=1) + dskip_ref[...] * xs

    yz = y * _silu(z_ref[rows, :])
    normed = []
    for g in range(SSD_GROUPS):
        part = yz[:, g * GROUP_DIM:(g + 1) * GROUP_DIM]
        normed.append(part * lax.rsqrt(jnp.mean(part * part, axis=-1, keepdims=True) + EPS))
    ssd_out = jnp.concatenate(normed, axis=1) * ng_ref[...]
    o_ref[rows, A_DIM:A_DIM + SSD_DIM] = ssd_out.astype(BF16)

    v = _gelu(v_ref[rows, :])
    vc = v - jnp.mean(v, axis=-1, keepdims=True)
    vn = vc * lax.rsqrt(jnp.mean(vc * vc, axis=-1, keepdims=True) + EPS) * lg_ref[...] + lb_ref[...]
    vn = vn.astype(BF16)
    for h in range(A_HEADS):
        hs = slice(h * A_HEAD_DIM, (h + 1) * A_HEAD_DIM)
        s = _dot(ws_ref[h], vn[:, hs]) + bs_ref[:, hs]
        o_ref[rows, hs] = (_gelu(u_ref[rows, hs]) * s).astype(BF16)


def _mixer(p, act, dt, prev_f, prev_b, consts, extra, *, nb, nc):
    nsteps = nc // CHUNKS_PER_STEP
    lane_blk = lambda k: pl.BlockSpec((STEP_ROWS, A_DIM), lambda b, c: (b * nsteps + c, k))
    step_rows = lambda width: pl.BlockSpec((STEP_ROWS, width), lambda b, c: (b * nsteps + c, 0))
    state_spec = pl.BlockSpec((CHUNKS_PER_STEP, SSD_STATE, SSD_DIM),
                              lambda b, c: (b * nsteps + c, 0, 0))
    params = tuple(consts[2:]) + tuple(extra)
    args = (p, p, p, act, dt, prev_f, prev_b) + params
    in_specs = [lane_blk(0), lane_blk(1), lane_blk(2), step_rows(XBC_DIM), step_rows(LANES),
                state_spec, state_spec]
    in_specs += [_full_spec(a) for a in params]
    return pl.pallas_call(
        _mixer_kernel,
        grid=(nb, nsteps),
        in_specs=in_specs,
        out_specs=step_rows(A_DIM + SSD_DIM),
        out_shape=jax.ShapeDtypeStruct((nb * nc * CHUNK, A_DIM + SSD_DIM), BF16),
        compiler_params=_params(("arbitrary", "arbitrary")),
        name="mixer",
    )(*args)


def _outproj_kernel(mix_ref, x_ref, gate_ref, w_ref, g_ref, shift_ref, scale_ref, o_ref, h_ref):
    r = x_ref[...] + gate_ref[...] * _dot(mix_ref[...], w_ref[...])
    o_ref[...] = r
    h_ref[...] = _norm_mod(r, g_ref[...], shift_ref[...], scale_ref[...])


def _outproj(mix, x2d, gate, w, g, shift, scale, *, rows_per_mod, tm=512):
    t = x2d.shape[0]
    tiles_per_mod = rows_per_mod // tm
    tile = pl.BlockSpec((tm, D_MODEL), lambda i: (i, 0))
    mod_spec = pl.BlockSpec((None, 1, D_MODEL), lambda i: (i // tiles_per_mod, 0, 0))
    return pl.pallas_call(
        _outproj_kernel,
        grid=(t // tm,),
        in_specs=[tile, tile, mod_spec, pl.BlockSpec((D_MODEL, D_MODEL), lambda i: (0, 0)),
                  pl.BlockSpec((1, D_MODEL), lambda i: (0, 0)), mod_spec, mod_spec],
        out_specs=[tile, tile],
        out_shape=[jax.ShapeDtypeStruct((t, D_MODEL), F32), jax.ShapeDtypeStruct((t, D_MODEL), BF16)],
        compiler_params=_params(("arbitrary",)),
        name="outproj",
    )(mix, x2d, gate, w, g, shift, scale)


def _ffn_weights(w_gate, w_up, w_down):
    return w_gate.astype(BF16), w_up.astype(BF16), w_down.astype(BF16)


def _ssd_consts(conv_w, conv_b, dt_bias, a_log):
    lanes_pad = LANES - 2 * SSD_HEADS
    bias_row = jnp.pad(dt_bias.reshape(1, 2 * SSD_HEADS), ((0, 0), (0, lanes_pad)))
    a_row = jnp.pad(-jnp.exp(a_log.reshape(1, 2 * SSD_HEADS)), ((0, 0), (0, lanes_pad)))
    tl = jnp.asarray(np.tril(np.ones((CHUNK, CHUNK), np.float32)))
    head_of_lane = np.arange(SSD_DIM) // SSD_HEAD_DIM
    sel = np.arange(LANES)[:, None] == head_of_lane[None, :]
    sel_b = np.arange(LANES)[:, None] == (head_of_lane[None, :] + SSD_HEADS)
    ef = jnp.asarray(sel.astype(np.float32)).astype(BF16)
    eb = jnp.asarray(sel_b.astype(np.float32)).astype(BF16)
    return (conv_w, conv_b.reshape(1, XBC_DIM), bias_row, a_row, tl, tl.T, ef, eb)


def kernel(x, c, ctx, c_ctx, w_mod, b_mod, norm_ffn1, ffn1_w_gate, ffn1_w_up, ffn1_w_down,
           norm_mix, w_in, conv_w, conv_b, dt_bias, a_log, d_skip, ssd_norm_g, gmlp_norm_g,
           gmlp_norm_b, gmlp_w_s, gmlp_b_s, w_out, norm_ffn2, ffn2_w_gate, ffn2_w_up,
           ffn2_w_down, norm_final):
    assert w_mod.shape[0] == 1, "single-layer block"
    bsz, seq, _ = x.shape
    ctx_len = ctx.shape[1]
    nc_x = seq // CHUNK
    nc_c = ctx_len // CHUNK

    pad_rows = (-(bsz + 1)) % SUBLANES
    c_all = jnp.concatenate([c, c_ctx[None, :], jnp.zeros((pad_rows, D_MODEL), F32)], axis=0)
    mod = _modulation(c_all, w_mod[0], b_mod[0]).reshape(-1, N_MOD, D_MODEL)
    mod_x = [mod[:bsz, k][:, None, :] for k in range(N_MOD)]
    mod_c = [mod[bsz:bsz + 1, k][:, None, :] for k in range(N_MOD)]

    row = lambda a: a.reshape(1, -1)
    x2d = x.reshape(bsz * seq, D_MODEL)
    c2d = ctx.reshape(bsz * ctx_len, D_MODEL)

    w1 = _ffn_weights(ffn1_w_gate[0], ffn1_w_up[0], ffn1_w_down[0])
    g1 = row(norm_ffn1[0])
    x1 = _ffn(x2d, None, mod_x[0], mod_x[1], mod_x[2], g1, *w1, g1, rows_per_mod=seq,
              final_norm=False)
    c1 = _ffn(c2d, None, mod_c[0], mod_c[1], mod_c[2], g1, *w1, g1, rows_per_mod=bsz * ctx_len,
              final_norm=False)

    w_main = w_in[0][:, :MAIN_COLS].astype(BF16)
    w_dt = jnp.pad(w_in[0][:, MAIN_COLS:], ((0, 0), (0, LANES - 2 * SSD_HEADS))).astype(BF16)
    gm = row(norm_mix[0])
    px, dtx = _inproj(x1, mod_x[3], mod_x[4], gm, w_main, w_dt, rows_per_mod=seq)
    pc, dtc = _inproj(c1, mod_c[3], mod_c[4], gm, w_main[:, MAIN_COLS - XBC_DIM:], w_dt,
                      rows_per_mod=bsz * ctx_len)

    consts = _ssd_consts(conv_w[0], conv_b[0], dt_bias[0], a_log[0])
    h0 = jnp.zeros((bsz, SSD_STATE, SSD_DIM), F32)
    _, _, _, hf, hb = _scan(pc, dtc, h0, h0, consts, nb=bsz, nc=nc_c, col=0)
    act, prev_f, prev_b, _, _ = _scan(px, dtx, hf, hb, consts, nb=bsz, nc=nc_x,
                                      col=(MAIN_COLS - XBC_DIM) // XBC_DIM)

    extra = (row(jnp.repeat(d_skip[0], SSD_HEAD_DIM)), row(ssd_norm_g[0]), row(gmlp_norm_g[0]),
             row(gmlp_norm_b[0]), gmlp_w_s[0].astype(BF16),
             jnp.repeat(gmlp_b_s[0].T, A_HEAD_DIM, axis=1))
    mix = _mixer(px, act, dtx, prev_f, prev_b, consts, extra, nb=bsz, nc=nc_x)
    x2, h2 = _outproj(mix, x1, mod_x[5], w_out[0].astype(BF16), row(norm_ffn2[0]), mod_x[6],
                      mod_x[7], rows_per_mod=seq)

    w2 = _ffn_weights(ffn2_w_gate[0], ffn2_w_up[0], ffn2_w_down[0])
    out = _ffn(x2, h2, None, None, mod_x[8], None, *w2, row(norm_final), rows_per_mod=seq,
               final_norm=True)
    return out.reshape(bsz, seq, D_MODEL)
```

```python
import functools
import math

import jax
import jax.numpy as jnp
import numpy as np
from jax import lax
from jax.experimental import pallas as pl
from jax.experimental.pallas import tpu as pltpu

F32 = jnp.float32
BF16 = jnp.bfloat16

D_MODEL = 2048
N_MOD = 9
D_FF = 5504
A_HEADS = 8
A_HEAD_DIM = 128
A_DIM = A_HEADS * A_HEAD_DIM
SSD_HEADS = 16
SSD_HEAD_DIM = 64
SSD_DIM = SSD_HEADS * SSD_HEAD_DIM
SSD_GROUPS = 2
SSD_STATE = 128
GROUP_DIM = SSD_DIM // SSD_GROUPS
CHUNK = 128
CONV_W = 5
XBC_DIM = SSD_DIM + 2 * SSD_GROUPS * SSD_STATE
MAIN_COLS = 2 * A_DIM + SSD_DIM + XBC_DIM
EPS = 1e-6

LANES = 128
SUBLANES = 8
VMEM_LIMIT_BYTES = 60 * 1024 * 1024

HALO = SUBLANES
PAD_ROWS = CHUNK + 2 * HALO


def _params(sem):
    return pltpu.CompilerParams(dimension_semantics=sem, vmem_limit_bytes=VMEM_LIMIT_BYTES)


def _dot(a, b):
    return jnp.dot(a, b, preferred_element_type=F32)


def _silu(x):
    return x * jax.nn.sigmoid(x)


def _gelu(x):
    return 0.5 * x * (1.0 + lax.erf(x * (1.0 / math.sqrt(2.0))))


def _softplus(x):
    return jnp.maximum(x, 0.0) + jnp.log1p(jnp.exp(-jnp.abs(x)))


def _rms(x, g):
    return x * lax.rsqrt(jnp.mean(x * x, axis=-1, keepdims=True) + EPS) * g


def _mod_kernel(c_ref, w_ref, b_ref, o_ref):
    o_ref[...] = _dot(_silu(c_ref[...]), w_ref[...]) + b_ref[...]


def _modulation(c_all, w_mod, b_mod, tn=1024):
    rows = c_all.shape[0]
    n = w_mod.shape[1]
    return pl.pallas_call(
        _mod_kernel,
        grid=(n // tn,),
        in_specs=[pl.BlockSpec((rows, D_MODEL), lambda j: (0, 0)),
                  pl.BlockSpec((D_MODEL, tn), lambda j: (0, j)),
                  pl.BlockSpec((1, tn), lambda j: (0, j))],
        out_specs=pl.BlockSpec((rows, tn), lambda j: (0, j)),
        out_shape=jax.ShapeDtypeStruct((rows, n), F32),
        compiler_params=_params(("arbitrary",)),
        name="mod",
    )(c_all, w_mod, b_mod.reshape(1, n))


FFN_TM = 512
FFN_TF = 1024


def _norm_mod(x, g, shift, scale):
    return (_rms(x, g) * (1.0 + scale) + shift).astype(BF16)


def _ffn_kernel(*refs, final_norm, last_cols, pre_normed):
    if pre_normed:
        x_ref, h_ref, gate_ref, wg_ref, wu_ref, wd_ref, gf_ref, o_ref = refs
    else:
        (x_ref, shift_ref, scale_ref, gate_ref, g_ref, wg_ref, wu_ref, wd_ref, gf_ref,
         o_ref, h_ref) = refs
    j = pl.program_id(1)
    last = pl.num_programs(1) - 1

    if not pre_normed:
        @pl.when(j == 0)
        def _():
            h_ref[...] = _norm_mod(x_ref[...], g_ref[...], shift_ref[...], scale_ref[...])

    def partial_ffn(cols):
        h = h_ref[...]
        gate = _dot(h, wg_ref[:, 0:cols])
        up = _dot(h, wu_ref[:, 0:cols])
        a = (_silu(gate) * up).astype(BF16)
        return _dot(a, wd_ref[0:cols, :])

    @pl.when(j == 0)
    def _():
        o_ref[...] = partial_ffn(FFN_TF)

    @pl.when((j > 0) & (j < last))
    def _():
        o_ref[...] += partial_ffn(FFN_TF)

    @pl.when(j == last)
    def _():
        f = o_ref[...] + partial_ffn(last_cols)
        r = x_ref[...] + (0.5 * gate_ref[...]) * f
        if final_norm:
            r = _rms(r, gf_ref[...])
        o_ref[...] = r


def _ffn(x2d, h2d, shift, scale, gate, g, wg, wu, wd, gf, *, rows_per_mod, final_norm):
    t = x2d.shape[0]
    tm, tf = FFN_TM, FFN_TF
    nf = pl.cdiv(D_FF, tf)
    assert nf >= 3
    pre_normed = h2d is not None
    tiles_per_mod = rows_per_mod // tm
    mod_spec = pl.BlockSpec((None, 1, D_MODEL), lambda i, j: (i // tiles_per_mod, 0, 0))
    row_spec = pl.BlockSpec((1, D_MODEL), lambda i, j: (0, 0))
    tile_spec = pl.BlockSpec((tm, D_MODEL), lambda i, j: (i, 0))
    w_specs = [pl.BlockSpec((None, D_MODEL, tf), lambda i, j: (j, 0, 0)),
               pl.BlockSpec((None, D_MODEL, tf), lambda i, j: (j, 0, 0)),
               pl.BlockSpec((tf, D_MODEL), lambda i, j: (j, 0))]
    if pre_normed:
        args = (x2d, h2d, gate, wg, wu, wd, gf)
        in_specs = [tile_spec, tile_spec, mod_spec] + w_specs + [row_spec]
        scratch = []
    else:
        args = (x2d, shift, scale, gate, g, wg, wu, wd, gf)
        in_specs = [tile_spec, mod_spec, mod_spec, mod_spec, row_spec] + w_specs + [row_spec]
        scratch = [pltpu.VMEM((tm, D_MODEL), BF16)]
    return pl.pallas_call(
        functools.partial(_ffn_kernel, final_norm=final_norm, last_cols=D_FF - (nf - 1) * tf,
                          pre_normed=pre_normed),
        grid=(t // tm, nf),
        in_specs=in_specs,
        out_specs=tile_spec,
        out_shape=jax.ShapeDtypeStruct((t, D_MODEL), F32),
        scratch_shapes=scratch,
        compiler_params=_params(("arbitrary", "arbitrary")),
        name="ffn_final" if final_norm else "ffn",
    )(*args)


def _inproj_kernel(x_ref, shift_ref, scale_ref, g_ref, w_ref, wdt_ref, p_ref, dt_ref, h_ref):
    @pl.when(pl.program_id(1) == 0)
    def _():
        h_ref[...] = _norm_mod(x_ref[...], g_ref[...], shift_ref[...], scale_ref[...])
        p_ref[...] = _dot(h_ref[...], w_ref[...])
        dt_ref[...] = _dot(h_ref[...], wdt_ref[...])

    @pl.when(pl.program_id(1) > 0)
    def _():
        p_ref[...] = _dot(h_ref[...], w_ref[...])


def _inproj(x2d, shift, scale, g, w, *, rows_per_mod, first_block, n_blocks, tn=XBC_DIM):
    t = x2d.shape[0]
    tm = min(1024, rows_per_mod)
    tiles_per_mod = rows_per_mod // tm
    mod_spec = pl.BlockSpec((None, 1, D_MODEL), lambda i, j: (i // tiles_per_mod, 0, 0))
    return pl.pallas_call(
        _inproj_kernel,
        grid=(t // tm, n_blocks),
        in_specs=[pl.BlockSpec((tm, D_MODEL), lambda i, j: (i, 0)), mod_spec, mod_spec,
                  pl.BlockSpec((1, D_MODEL), lambda i, j: (0, 0)),
                  pl.BlockSpec((D_MODEL, tn), lambda i, j: (0, first_block + j)),
                  pl.BlockSpec((D_MODEL, LANES), lambda i, j: (0, MAIN_COLS // LANES))],
        out_specs=[pl.BlockSpec((tm, tn), lambda i, j: (i, j)),
                   pl.BlockSpec((tm, LANES), lambda i, j: (i, 0))],
        out_shape=[jax.ShapeDtypeStruct((t, n_blocks * tn), F32),
                   jax.ShapeDtypeStruct((t, LANES), F32)],
        scratch_shapes=[pltpu.VMEM((tm, D_MODEL), BF16)],
        compiler_params=_params(("arbitrary", "arbitrary")),
        name="inproj",
    )(x2d, shift, scale, g, w, w)


def _conv_silu(cur_ref, r0, prev_rows, next_rows, w_ref, b_ref, pad_ref, sub):
    outs = []
    for s in range(XBC_DIM // LANES):
        ls = slice(s * LANES, (s + 1) * LANES)
        pad_ref[sub, s, 0:HALO, :] = prev_rows(ls)
        pad_ref[sub, s, HALO:HALO + CHUNK, :] = cur_ref[r0:r0 + CHUNK, ls]
        pad_ref[sub, s, HALO + CHUNK:PAD_ROWS, :] = next_rows(ls)
        acc = jnp.broadcast_to(b_ref[:, ls], (CHUNK, LANES))
        for k in range(CONV_W):
            start = HALO - CONV_W // 2 + k
            acc = acc + pad_ref[sub, s, pl.ds(start, CHUNK, stride=1), :] * w_ref[k:k + 1, ls]
        outs.append(_silu(acc))
    return jnp.concatenate(outs, axis=1)


def _dt_terms(dt_raw, bias_row, a_row, tl, tu):
    dt = _softplus(dt_raw + bias_row)
    a = dt * a_row
    s = jnp.dot(tl, a, precision=lax.Precision.HIGHEST, preferred_element_type=F32)
    r = jnp.dot(tu, a, precision=lax.Precision.HIGHEST, preferred_element_type=F32)
    is_fwd = lax.broadcasted_iota(jnp.int32, (CHUNK, LANES), 1) < SSD_HEADS
    cum = jnp.where(is_fwd, s, r)
    total = jnp.where(is_fwd[0:1], s[CHUNK - 1:CHUNK], r[0:1])
    return dt, cum, total


def _expand(v, e):
    hi = v.astype(BF16)
    lo = (v - hi.astype(F32)).astype(BF16)
    return _dot(hi, e) + _dot(lo, e)


CHUNKS_PER_STEP = 2
STEP_ROWS = CHUNKS_PER_STEP * CHUNK


def _scan_kernel(xc_ref, xp_ref, xn_ref, dt_ref, h0f_ref, h0b_ref, cw_ref, cb_ref, bias_ref,
                 arow_ref, tl_ref, tu_ref, ef_ref, eb_ref, act_ref, pf_ref, pb_ref, ff_ref, fb_ref,
                 pad_ref, sf_ref, sball_ref, dball_ref, *, nc):
    c = pl.program_id(1)
    nsteps = nc // CHUNKS_PER_STEP

    @pl.when(c == 0)
    def _():
        sf_ref[...] = h0f_ref[...]

    state = sf_ref[...]
    for sub in range(CHUNKS_PER_STEP):
        r0 = sub * CHUNK
        if sub == 0:
            prev_rows = lambda ls: jnp.where(c == 0, 0.0, xp_ref[:, ls])
        else:
            prev_rows = lambda ls, r0=r0: xc_ref[r0 - HALO:r0, ls]
        if sub == CHUNKS_PER_STEP - 1:
            next_rows = lambda ls: jnp.where(c == nsteps - 1, 0.0, xn_ref[:, ls])
        else:
            next_rows = lambda ls, r0=r0: xc_ref[r0 + CHUNK:r0 + CHUNK + HALO, ls]
        xb = _conv_silu(xc_ref, r0, prev_rows, next_rows, cw_ref, cb_ref, pad_ref, sub)
        act_ref[r0:r0 + CHUNK, :] = xb
        dt, cum, total = _dt_terms(dt_ref[r0:r0 + CHUNK, :], bias_ref[...], arow_ref[...],
                                   tl_ref[...], tu_ref[...])
        w = dt * jnp.exp(total - cum)
        decay = jnp.broadcast_to(jnp.exp(total), (SUBLANES, LANES))
        v = jnp.concatenate([w, decay], axis=0)
        xs = xb[:, 0:SSD_DIM]
        bt = []
        for g in range(SSD_GROUPS):
            lo = SSD_DIM + g * SSD_STATE
            bt.append(xb[:, lo:lo + SSD_STATE].T.astype(BF16))

        def chunk_state(e_ref):
            ex = _expand(v, e_ref[...])
            xw = (xs * ex[0:CHUNK]).astype(BF16)
            new = [_dot(bt[g], xw[:, g * GROUP_DIM:(g + 1) * GROUP_DIM])
                   for g in range(SSD_GROUPS)]
            return jnp.concatenate(new, axis=1), ex[CHUNK:CHUNK + SUBLANES]

        new_f, decay_f = chunk_state(ef_ref)
        pf_ref[sub] = state.astype(BF16)
        state = state * decay_f[0:1] + new_f

        new_b, decay_b = chunk_state(eb_ref)
        sball_ref[c * CHUNKS_PER_STEP + sub] = new_b
        dball_ref[c * CHUNKS_PER_STEP + sub] = decay_b
    sf_ref[...] = state

    @pl.when(c == nsteps - 1)
    def _():
        ff_ref[...] = sf_ref[...]
        sb = h0b_ref[...]
        for cc in reversed(range(nc)):
            pb_ref[cc] = sb.astype(BF16)
            sb = sb * dball_ref[cc, 0:1, :] + sball_ref[cc]
        fb_ref[...] = sb


def _halo_specs(nsteps, col):
    per = STEP_ROWS // HALO
    cur = pl.BlockSpec((STEP_ROWS, XBC_DIM), lambda b, c: (b * nsteps + c, col))
    prev = pl.BlockSpec((HALO, XBC_DIM),
                        lambda b, c: (jnp.maximum((b * nsteps + c) * per - 1, 0), col))
    nxt = pl.BlockSpec((HALO, XBC_DIM),
                       lambda b, c: (jnp.minimum((b * nsteps + c + 1) * per,
                                                 (b * nsteps + nsteps) * per - 1), col))
    return cur, prev, nxt


def _full_spec(a):
    return pl.BlockSpec(a.shape, lambda b, c: (0,) * a.ndim)


def _scan(p, dt, h0f, h0b, consts, *, nb, nc, col):
    assert nc % CHUNKS_PER_STEP == 0
    nsteps = nc // CHUNKS_PER_STEP
    state_spec = pl.BlockSpec((None, SSD_STATE, SSD_DIM), lambda b, c: (b, 0, 0))
    step_rows = lambda width: pl.BlockSpec((STEP_ROWS, width), lambda b, c: (b * nsteps + c, 0))
    args = (p, p, p, dt, h0f, h0b) + tuple(consts)
    in_specs = list(_halo_specs(nsteps, col)) + [step_rows(LANES), state_spec, state_spec]
    in_specs += [_full_spec(a) for a in consts]
    return pl.pallas_call(
        functools.partial(_scan_kernel, nc=nc),
        grid=(nb, nsteps),
        in_specs=in_specs,
        out_specs=[step_rows(XBC_DIM),
                   pl.BlockSpec((CHUNKS_PER_STEP, SSD_STATE, SSD_DIM),
                                lambda b, c: (b * nsteps + c, 0, 0)),
                   pl.BlockSpec((nc, SSD_STATE, SSD_DIM), lambda b, c: (b, 0, 0)),
                   state_spec, state_spec],
        out_shape=[jax.ShapeDtypeStruct((nb * nc * CHUNK, XBC_DIM), F32),
                   jax.ShapeDtypeStruct((nb * nc, SSD_STATE, SSD_DIM), BF16),
                   jax.ShapeDtypeStruct((nb * nc, SSD_STATE, SSD_DIM), BF16),
                   jax.ShapeDtypeStruct((nb, SSD_STATE, SSD_DIM), F32),
                   jax.ShapeDtypeStruct((nb, SSD_STATE, SSD_DIM), F32)],
        scratch_shapes=[pltpu.VMEM((CHUNKS_PER_STEP, XBC_DIM // LANES, PAD_ROWS, LANES), F32),
                        pltpu.VMEM((SSD_STATE, SSD_DIM), F32),
                        pltpu.VMEM((nc, SSD_STATE, SSD_DIM), F32),
                        pltpu.VMEM((nc, SUBLANES, SSD_DIM), F32)],
        compiler_params=_params(("arbitrary", "arbitrary")),
        name="ssd_scan",
    )(*args)


def _mixer_kernel(u_ref, v_ref, z_ref, act_ref, dt_ref, pf_ref, pb_ref,
                  bias_ref, arow_ref, tl_ref, tu_ref, ef_ref, eb_ref,
                  dskip_ref, ng_ref, lg_ref, lb_ref, ws_ref, bs_ref, o_ref):
    for sub in range(CHUNKS_PER_STEP):
        _mixer_chunk(sub, u_ref, v_ref, z_ref, act_ref, dt_ref, pf_ref, pb_ref, bias_ref, arow_ref,
                     tl_ref, tu_ref, ef_ref, eb_ref, dskip_ref, ng_ref, lg_ref, lb_ref, ws_ref,
                     bs_ref, o_ref)


def _mixer_chunk(sub, u_ref, v_ref, z_ref, act_ref, dt_ref, pf_ref, pb_ref, bias_ref, arow_ref,
                 tl_ref, tu_ref, ef_ref, eb_ref, dskip_ref, ng_ref, lg_ref, lb_ref, ws_ref, bs_ref,
                 o_ref):
    rows = slice(sub * CHUNK, (sub + 1) * CHUNK)
    xs = act_ref[rows, 0:SSD_DIM]
    gn = SSD_GROUPS * SSD_STATE
    bm = act_ref[rows, SSD_DIM:SSD_DIM + gn].astype(BF16)
    cm = act_ref[rows, SSD_DIM + gn:XBC_DIM].astype(BF16)

    dt, cum, _ = _dt_terms(dt_ref[rows, :], bias_ref[...], arow_ref[...], tl_ref[...], tu_ref[...])
    ecum = jnp.exp(cum)
    ecum_f = _expand(ecum, ef_ref[...])
    ecum_b = _expand(ecum, eb_ref[...])
    dt_t = dt.T
    cum_t = cum.T

    lane = lax.broadcasted_iota(jnp.int32, (CHUNK, SSD_DIM), 1)
    first_head = (lane % LANES) < SSD_HEAD_DIM
    xs_even = jnp.where(first_head, xs, 0.0).astype(BF16)
    xs_odd = jnp.where(first_head, 0.0, xs).astype(BF16)

    row = lax.broadcasted_iota(jnp.int32, (CHUNK, CHUNK), 0)
    col = lax.broadcasted_iota(jnp.int32, (CHUNK, CHUNK), 1)
    causal = row >= col
    anticausal = row <= col

    def decay_matrix(h):
        lf = jnp.exp(jnp.minimum(cum[:, h:h + 1] - cum_t[h:h + 1, :], 0.0))
        lf = jnp.where(causal, lf, 0.0) * dt_t[h:h + 1, :]
        hb = SSD_HEADS + h
        lb = jnp.exp(jnp.minimum(cum[:, hb:hb + 1] - cum_t[hb:hb + 1, :], 0.0))
        lb = jnp.where(anticausal, lb, 0.0) * dt_t[hb:hb + 1, :]
        return lf + lb

    pf = pf_ref[sub]
    pb = pb_ref[sub]
    heads_per_group = SSD_HEADS // SSD_GROUPS
    y_parts = []
    for g in range(SSD_GROUPS):
        cg = cm[:, g * SSD_STATE:(g + 1) * SSD_STATE]
        bg = bm[:, g * SSD_STATE:(g + 1) * SSD_STATE]
        cb_g = lax.dot_general(cg, bg, (((1,), (1,)), ((), ())), preferred_element_type=F32)
        gs = slice(g * GROUP_DIM, (g + 1) * GROUP_DIM)
        y_off = ecum_f[:, gs] * _dot(cg, pf[:, gs]) + ecum_b[:, gs] * _dot(cg, pb[:, gs])
        diag = []
        for j in range(heads_per_group // 2):
            h0 = g * heads_per_group + 2 * j
            m_pair = jnp.concatenate([(cb_g * decay_matrix(h0)).astype(BF16),
                                      (cb_g * decay_matrix(h0 + 1)).astype(BF16)], axis=1)
            ls = slice(h0 * SSD_HEAD_DIM, (h0 + 2) * SSD_HEAD_DIM)
            rhs = jnp.concatenate([xs_even[:, ls], xs_odd[:, ls]], axis=0)
            diag.append(_dot(m_pair, rhs))
        y_parts.append(jnp.concatenate(diag, axis=1) + y_off)
    y = jnp.concatenate(y_parts, axis=1) + dskip_ref[...] * xs

    yz = y * _silu(z_ref[rows, :])
    normed = []
    for g in range(SSD_GROUPS):
        part = yz[:, g * GROUP_DIM:(g + 1) * GROUP_DIM]
        normed.append(part * lax.rsqrt(jnp.mean(part * part, axis=-1, keepdims=True) + EPS))
    ssd_out = jnp.concatenate(normed, axis=1) * ng_ref[...]
    o_ref[rows, A_DIM:A_DIM + SSD_DIM] = ssd_out.astype(BF16)

    v = _gelu(v_ref[rows, :])
    vc = v - jnp.mean(v, axis=-1, keepdims=True)
    vn = vc * lax.rsqrt(jnp.mean(vc * vc, axis=-1, keepdims=True) + EPS) * lg_ref[...] + lb_ref[...]
    vn = vn.astype(BF16)
    for h in range(A_HEADS):
        hs = slice(h * A_HEAD_DIM, (h + 1) * A_HEAD_DIM)
        s = _dot(ws_ref[h], vn[:, hs]) + bs_ref[:, hs]
        o_ref[rows, hs] = (_gelu(u_ref[rows, hs]) * s).astype(BF16)


def _mixer(p, act, dt, prev_f, prev_b, consts, extra, *, nb, nc):
    nsteps = nc // CHUNKS_PER_STEP
    lane_blk = lambda k: pl.BlockSpec((STEP_ROWS, A_DIM), lambda b, c: (b * nsteps + c, k))
    step_rows = lambda width: pl.BlockSpec((STEP_ROWS, width), lambda b, c: (b * nsteps + c, 0))
    state_spec = pl.BlockSpec((CHUNKS_PER_STEP, SSD_STATE, SSD_DIM),
                              lambda b, c: (b * nsteps + c, 0, 0))
    params = tuple(consts[2:]) + tuple(extra)
    args = (p, p, p, act, dt, prev_f, prev_b) + params
    in_specs = [lane_blk(0), lane_blk(1), lane_blk(2), step_rows(XBC_DIM), step_rows(LANES),
                state_spec, state_spec]
    in_specs += [_full_spec(a) for a in params]
    return pl.pallas_call(
        _mixer_kernel,
        grid=(nb, nsteps),
        in_specs=in_specs,
        out_specs=step_rows(A_DIM + SSD_DIM),
        out_shape=jax.ShapeDtypeStruct((nb * nc * CHUNK, A_DIM + SSD_DIM), BF16),
        compiler_params=_params(("arbitrary", "arbitrary")),
        name="mixer",
    )(*args)


def _outproj_kernel(mix_ref, x_ref, gate_ref, w_ref, g_ref, shift_ref, scale_ref, o_ref, h_ref):
    r = x_ref[...] + gate_ref[...] * _dot(mix_ref[...], w_ref[...])
    o_ref[...] = r
    h_ref[...] = _norm_mod(r, g_ref[...], shift_ref[...], scale_ref[...])


def _outproj(mix, x2d, gate, w, g, shift, scale, *, rows_per_mod, tm=512):
    t = x2d.shape[0]
    tiles_per_mod = rows_per_mod // tm
    tile = pl.BlockSpec((tm, D_MODEL), lambda i: (i, 0))
    mod_spec = pl.BlockSpec((None, 1, D_MODEL), lambda i: (i // tiles_per_mod, 0, 0))
    return pl.pallas_call(
        _outproj_kernel,
        grid=(t // tm,),
        in_specs=[tile, tile, mod_spec, pl.BlockSpec((D_MODEL, D_MODEL), lambda i: (0, 0)),
                  pl.BlockSpec((1, D_MODEL), lambda i: (0, 0)), mod_spec, mod_spec],
        out_specs=[tile, tile],
        out_shape=[jax.ShapeDtypeStruct((t, D_MODEL), F32), jax.ShapeDtypeStruct((t, D_MODEL), BF16)],
        compiler_params=_params(("arbitrary",)),
        name="outproj",
    )(mix, x2d, gate, w, g, shift, scale)


def _ffn_weights(w_gate, w_up, w_down):
    nf = pl.cdiv(D_FF, FFN_TF)

    def blocks(w):
        w = jnp.pad(w.astype(BF16), ((0, 0), (0, nf * FFN_TF - D_FF)))
        return w.reshape(D_MODEL, nf, FFN_TF).transpose(1, 0, 2)

    return blocks(w_gate), blocks(w_up), w_down.astype(BF16)


def _ssd_consts(conv_w, conv_b, dt_bias, a_log):
    lanes_pad = LANES - 2 * SSD_HEADS
    bias_row = jnp.pad(dt_bias.reshape(1, 2 * SSD_HEADS), ((0, 0), (0, lanes_pad)))
    a_row = jnp.pad(-jnp.exp(a_log.reshape(1, 2 * SSD_HEADS)), ((0, 0), (0, lanes_pad)))
    tl = jnp.asarray(np.tril(np.ones((CHUNK, CHUNK), np.float32)))
    head_of_lane = np.arange(SSD_DIM) // SSD_HEAD_DIM
    sel = np.arange(LANES)[:, None] == head_of_lane[None, :]
    sel_b = np.arange(LANES)[:, None] == (head_of_lane[None, :] + SSD_HEADS)
    ef = jnp.asarray(sel.astype(np.float32)).astype(BF16)
    eb = jnp.asarray(sel_b.astype(np.float32)).astype(BF16)
    return (conv_w, conv_b.reshape(1, XBC_DIM), bias_row, a_row, tl, tl.T, ef, eb)


def kernel(x, c, ctx, c_ctx, w_mod, b_mod, norm_ffn1, ffn1_w_gate, ffn1_w_up, ffn1_w_down,
           norm_mix, w_in, conv_w, conv_b, dt_bias, a_log, d_skip, ssd_norm_g, gmlp_norm_g,
           gmlp_norm_b, gmlp_w_s, gmlp_b_s, w_out, norm_ffn2, ffn2_w_gate, ffn2_w_up,
           ffn2_w_down, norm_final):
    assert w_mod.shape[0] == 1, "single-layer block"
    bsz, seq, _ = x.shape
    ctx_len = ctx.shape[1]
    nc_x = seq // CHUNK
    nc_c = ctx_len // CHUNK

    pad_rows = (-(bsz + 1)) % SUBLANES
    c_all = jnp.concatenate([c, c_ctx[None, :], jnp.zeros((pad_rows, D_MODEL), F32)], axis=0)
    mod = _modulation(c_all, w_mod[0], b_mod[0]).reshape(-1, N_MOD, D_MODEL)
    mod_x = [mod[:bsz, k][:, None, :] for k in range(N_MOD)]
    mod_c = [mod[bsz:bsz + 1, k][:, None, :] for k in range(N_MOD)]

    row = lambda a: a.reshape(1, -1)
    x2d = x.reshape(bsz * seq, D_MODEL)
    c2d = ctx.reshape(bsz * ctx_len, D_MODEL)

    w1 = _ffn_weights(ffn1_w_gate[0], ffn1_w_up[0], ffn1_w_down[0])
    g1 = row(norm_ffn1[0])
    x1 = _ffn(x2d, None, mod_x[0], mod_x[1], mod_x[2], g1, *w1, g1, rows_per_mod=seq,
              final_norm=False)
    c1 = _ffn(c2d, None, mod_c[0], mod_c[1], mod_c[2], g1, *w1, g1, rows_per_mod=bsz * ctx_len,
              final_norm=False)

    w_proj = jnp.pad(w_in[0].astype(BF16), ((0, 0), (0, LANES - 2 * SSD_HEADS)))
    gm = row(norm_mix[0])
    main_blocks = MAIN_COLS // XBC_DIM
    px, dtx = _inproj(x1, mod_x[3], mod_x[4], gm, w_proj, rows_per_mod=seq, first_block=0,
                      n_blocks=main_blocks)
    pc, dtc = _inproj(c1, mod_c[3], mod_c[4], gm, w_proj, rows_per_mod=bsz * ctx_len,
                      first_block=main_blocks - 1, n_blocks=1)

    consts = _ssd_consts(conv_w[0], conv_b[0], dt_bias[0], a_log[0])
    h0 = jnp.zeros((bsz, SSD_STATE, SSD_DIM), F32)
    _, _, _, hf, hb = _scan(pc, dtc, h0, h0, consts, nb=bsz, nc=nc_c, col=0)
    act, prev_f, prev_b, _, _ = _scan(px, dtx, hf, hb, consts, nb=bsz, nc=nc_x,
                                      col=(MAIN_COLS - XBC_DIM) // XBC_DIM)

    extra = (row(jnp.repeat(d_skip[0], SSD_HEAD_DIM)), row(ssd_norm_g[0]), row(gmlp_norm_g[0]),
             row(gmlp_norm_b[0]), gmlp_w_s[0].astype(BF16),
             jnp.repeat(gmlp_b_s[0].T, A_HEAD_DIM, axis=1))
    mix = _mixer(px, act, dtx, prev_f, prev_b, consts, extra, nb=bsz, nc=nc_x)
    x2, h2 = _outproj(mix, x1, mod_x[5], w_out[0].astype(BF16), row(norm_ffn2[0]), mod_x[6],
                      mod_x[7], rows_per_mod=seq)

    w2 = _ffn_weights(ffn2_w_gate[0], ffn2_w_up[0], ffn2_w_down[0])
    out = _ffn(x2, h2, None, None, mod_x[8], None, *w2, row(norm_final), rows_per_mod=seq,
               final_norm=True)
    return out.reshape(bsz, seq, D_MODEL)
```

```python
import functools
import math

import jax
import jax.numpy as jnp
import numpy as np
from jax import lax
from jax.experimental import pallas as pl
from jax.experimental.pallas import tpu as pltpu

F32 = jnp.float32
BF16 = jnp.bfloat16

D_MODEL = 2048
N_MOD = 9
D_FF = 5504
A_HEADS = 8
A_HEAD_DIM = 128
A_DIM = A_HEADS * A_HEAD_DIM
SSD_HEADS = 16
SSD_HEAD_DIM = 64
SSD_DIM = SSD_HEADS * SSD_HEAD_DIM
SSD_GROUPS = 2
SSD_STATE = 128
GROUP_DIM = SSD_DIM // SSD_GROUPS
CHUNK = 128
CONV_W = 5
XBC_DIM = SSD_DIM + 2 * SSD_GROUPS * SSD_STATE
MAIN_COLS = 2 * A_DIM + SSD_DIM + XBC_DIM
EPS = 1e-6

LANES = 128
SUBLANES = 8
VMEM_LIMIT_BYTES = 60 * 1024 * 1024

HALO = SUBLANES
PAD_ROWS = CHUNK + 2 * HALO


def _params(sem):
    return pltpu.CompilerParams(dimension_semantics=sem, vmem_limit_bytes=VMEM_LIMIT_BYTES)


def _dot(a, b):
    return jnp.dot(a, b, preferred_element_type=F32)


def _silu(x):
    return x * jax.nn.sigmoid(x)


def _gelu(x):
    return 0.5 * x * (1.0 + lax.erf(x * (1.0 / math.sqrt(2.0))))


def _softplus(x):
    return jnp.maximum(x, 0.0) + jnp.log1p(jnp.exp(-jnp.abs(x)))


def _rms(x, g):
    return x * lax.rsqrt(jnp.mean(x * x, axis=-1, keepdims=True) + EPS) * g


def _mod_kernel(c_ref, w_ref, b_ref, o_ref):
    o_ref[...] = _dot(_silu(c_ref[...]), w_ref[...]) + b_ref[...]


def _modulation(c_all, w_mod, b_mod, tn=1024):
    rows = c_all.shape[0]
    n = w_mod.shape[1]
    return pl.pallas_call(
        _mod_kernel,
        grid=(n // tn,),
        in_specs=[pl.BlockSpec((rows, D_MODEL), lambda j: (0, 0)),
                  pl.BlockSpec((D_MODEL, tn), lambda j: (0, j)),
                  pl.BlockSpec((1, tn), lambda j: (0, j))],
        out_specs=pl.BlockSpec((rows, tn), lambda j: (0, j)),
        out_shape=jax.ShapeDtypeStruct((rows, n), F32),
        compiler_params=_params(("arbitrary",)),
        name="mod",
    )(c_all, w_mod, b_mod.reshape(1, n))


FFN_TM = 512
FFN_TF = 1024


def _norm_mod(x, g, shift, scale):
    return (_rms(x, g) * (1.0 + scale) + shift).astype(BF16)


def _ffn_kernel(*refs, final_norm, last_cols, pre_normed):
    if pre_normed:
        x_ref, h_ref, gate_ref, wg_ref, wu_ref, wd_ref, gf_ref, o_ref = refs
    else:
        (x_ref, shift_ref, scale_ref, gate_ref, g_ref, wg_ref, wu_ref, wd_ref, gf_ref,
         o_ref, h_ref) = refs
    j = pl.program_id(1)
    last = pl.num_programs(1) - 1

    if not pre_normed:
        @pl.when(j == 0)
        def _():
            h_ref[...] = _norm_mod(x_ref[...], g_ref[...], shift_ref[...], scale_ref[...])

    def partial_ffn(cols):
        h = h_ref[...]
        gate = _dot(h, wg_ref[:, 0:cols])
        up = _dot(h, wu_ref[:, 0:cols])
        a = (_silu(gate) * up).astype(BF16)
        return _dot(a, wd_ref[0:cols, :])

    @pl.when(j == 0)
    def _():
        o_ref[...] = partial_ffn(FFN_TF)

    @pl.when((j > 0) & (j < last))
    def _():
        o_ref[...] += partial_ffn(FFN_TF)

    @pl.when(j == last)
    def _():
        f = o_ref[...] + partial_ffn(last_cols)
        r = x_ref[...] + (0.5 * gate_ref[...]) * f
        if final_norm:
            r = _rms(r, gf_ref[...])
        o_ref[...] = r


def _ffn(x2d, h2d, shift, scale, gate, g, wg, wu, wd, gf, *, rows_per_mod, final_norm):
    t = x2d.shape[0]
    tm, tf = FFN_TM, FFN_TF
    nf = pl.cdiv(D_FF, tf)
    assert nf >= 3
    pre_normed = h2d is not None
    tiles_per_mod = rows_per_mod // tm
    mod_spec = pl.BlockSpec((None, 1, D_MODEL), lambda i, j: (i // tiles_per_mod, 0, 0))
    row_spec = pl.BlockSpec((1, D_MODEL), lambda i, j: (0, 0))
    tile_spec = pl.BlockSpec((tm, D_MODEL), lambda i, j: (i, 0))
    w_specs = [pl.BlockSpec((D_MODEL, tf), lambda i, j: (0, j)),
               pl.BlockSpec((D_MODEL, tf), lambda i, j: (0, j)),
               pl.BlockSpec((tf, D_MODEL), lambda i, j: (j, 0))]
    if pre_normed:
        args = (x2d, h2d, gate, wg, wu, wd, gf)
        in_specs = [tile_spec, tile_spec, mod_spec] + w_specs + [row_spec]
        scratch = []
    else:
        args = (x2d, shift, scale, gate, g, wg, wu, wd, gf)
        in_specs = [tile_spec, mod_spec, mod_spec, mod_spec, row_spec] + w_specs + [row_spec]
        scratch = [pltpu.VMEM((tm, D_MODEL), BF16)]
    return pl.pallas_call(
        functools.partial(_ffn_kernel, final_norm=final_norm, last_cols=D_FF - (nf - 1) * tf,
                          pre_normed=pre_normed),
        grid=(t // tm, nf),
        in_specs=in_specs,
        out_specs=tile_spec,
        out_shape=jax.ShapeDtypeStruct((t, D_MODEL), F32),
        scratch_shapes=scratch,
        compiler_params=_params(("arbitrary", "arbitrary")),
        name="ffn_final" if final_norm else "ffn",
    )(*args)


def _inproj_kernel(x_ref, shift_ref, scale_ref, g_ref, w_ref, wdt_ref, p_ref, dt_ref, h_ref):
    @pl.when(pl.program_id(1) == 0)
    def _():
        h_ref[...] = _norm_mod(x_ref[...], g_ref[...], shift_ref[...], scale_ref[...])
        p_ref[...] = _dot(h_ref[...], w_ref[...])
        dt_ref[...] = _dot(h_ref[...], wdt_ref[...])

    @pl.when(pl.program_id(1) > 0)
    def _():
        p_ref[...] = _dot(h_ref[...], w_ref[...])


def _inproj(x2d, shift, scale, g, w, *, rows_per_mod, first_block, n_blocks, tn=XBC_DIM):
    t = x2d.shape[0]
    tm = min(1024, rows_per_mod)
    tiles_per_mod = rows_per_mod // tm
    mod_spec = pl.BlockSpec((None, 1, D_MODEL), lambda i, j: (i // tiles_per_mod, 0, 0))
    return pl.pallas_call(
        _inproj_kernel,
        grid=(t // tm, n_blocks),
        in_specs=[pl.BlockSpec((tm, D_MODEL), lambda i, j: (i, 0)), mod_spec, mod_spec,
                  pl.BlockSpec((1, D_MODEL), lambda i, j: (0, 0)),
                  pl.BlockSpec((D_MODEL, tn), lambda i, j: (0, first_block + j)),
                  pl.BlockSpec((D_MODEL, LANES), lambda i, j: (0, MAIN_COLS // LANES))],
        out_specs=[pl.BlockSpec((tm, tn), lambda i, j: (i, j)),
                   pl.BlockSpec((tm, LANES), lambda i, j: (i, 0))],
        out_shape=[jax.ShapeDtypeStruct((t, n_blocks * tn), F32),
                   jax.ShapeDtypeStruct((t, LANES), F32)],
        scratch_shapes=[pltpu.VMEM((tm, D_MODEL), BF16)],
        compiler_params=_params(("arbitrary", "arbitrary")),
        name="inproj",
    )(x2d, shift, scale, g, w, w)


def _conv_silu(cur_ref, r0, prev_rows, next_rows, w_ref, b_ref, pad_ref, sub):
    outs = []
    for s in range(XBC_DIM // LANES):
        ls = slice(s * LANES, (s + 1) * LANES)
        pad_ref[sub, s, 0:HALO, :] = prev_rows(ls)
        pad_ref[sub, s, HALO:HALO + CHUNK, :] = cur_ref[r0:r0 + CHUNK, ls]
        pad_ref[sub, s, HALO + CHUNK:PAD_ROWS, :] = next_rows(ls)
        acc = jnp.broadcast_to(b_ref[:, ls], (CHUNK, LANES))
        for k in range(CONV_W):
            start = HALO - CONV_W // 2 + k
            acc = acc + pad_ref[sub, s, pl.ds(start, CHUNK, stride=1), :] * w_ref[k:k + 1, ls]
        outs.append(_silu(acc))
    return jnp.concatenate(outs, axis=1)


def _dt_terms(dt_raw, bias_row, a_row, tl, tu):
    dt = _softplus(dt_raw + bias_row)
    a = dt * a_row
    s = jnp.dot(tl, a, precision=lax.Precision.HIGHEST, preferred_element_type=F32)
    r = jnp.dot(tu, a, precision=lax.Precision.HIGHEST, preferred_element_type=F32)
    is_fwd = lax.broadcasted_iota(jnp.int32, (CHUNK, LANES), 1) < SSD_HEADS
    cum = jnp.where(is_fwd, s, r)
    total = jnp.where(is_fwd[0:1], s[CHUNK - 1:CHUNK], r[0:1])
    return dt, cum, total


def _expand(v, e):
    hi = v.astype(BF16)
    lo = (v - hi.astype(F32)).astype(BF16)
    return _dot(hi, e) + _dot(lo, e)


PROJ_COLS = 256
CHUNKS_PER_STEP = 2
STEP_ROWS = CHUNKS_PER_STEP * CHUNK


def _scan_kernel(xc_ref, xp_ref, xn_ref, dt_ref, h0f_ref, h0b_ref, cw_ref, cb_ref, bias_ref,
                 arow_ref, tl_ref, tu_ref, ef_ref, eb_ref, act_ref, pf_ref, pb_ref, ff_ref, fb_ref,
                 pad_ref, sf_ref, sball_ref, dball_ref, *, nc):
    c = pl.program_id(1)
    nsteps = nc // CHUNKS_PER_STEP

    @pl.when(c == 0)
    def _():
        sf_ref[...] = h0f_ref[...]

    state = sf_ref[...]
    for sub in range(CHUNKS_PER_STEP):
        r0 = sub * CHUNK
        if sub == 0:
            prev_rows = lambda ls: jnp.where(c == 0, 0.0, xp_ref[:, ls])
        else:
            prev_rows = lambda ls, r0=r0: xc_ref[r0 - HALO:r0, ls]
        if sub == CHUNKS_PER_STEP - 1:
            next_rows = lambda ls: jnp.where(c == nsteps - 1, 0.0, xn_ref[:, ls])
        else:
            next_rows = lambda ls, r0=r0: xc_ref[r0 + CHUNK:r0 + CHUNK + HALO, ls]
        xb = _conv_silu(xc_ref, r0, prev_rows, next_rows, cw_ref, cb_ref, pad_ref, sub)
        act_ref[r0:r0 + CHUNK, :] = xb
        dt, cum, total = _dt_terms(dt_ref[r0:r0 + CHUNK, :], bias_ref[...], arow_ref[...],
                                   tl_ref[...], tu_ref[...])
        w = dt * jnp.exp(total - cum)
        decay = jnp.broadcast_to(jnp.exp(total), (SUBLANES, LANES))
        v = jnp.concatenate([w, decay], axis=0)
        xs = xb[:, 0:SSD_DIM]
        bt = []
        for g in range(SSD_GROUPS):
            lo = SSD_DIM + g * SSD_STATE
            bt.append(xb[:, lo:lo + SSD_STATE].T.astype(BF16))

        def chunk_state(e_ref):
            ex = _expand(v, e_ref[...])
            xw = (xs * ex[0:CHUNK]).astype(BF16)
            new = [_dot(bt[g], xw[:, g * GROUP_DIM:(g + 1) * GROUP_DIM])
                   for g in range(SSD_GROUPS)]
            return jnp.concatenate(new, axis=1), ex[CHUNK:CHUNK + SUBLANES]

        new_f, decay_f = chunk_state(ef_ref)
        pf_ref[sub] = state.astype(BF16)
        state = state * decay_f[0:1] + new_f

        new_b, decay_b = chunk_state(eb_ref)
        sball_ref[c * CHUNKS_PER_STEP + sub] = new_b
        dball_ref[c * CHUNKS_PER_STEP + sub] = decay_b
    sf_ref[...] = state

    @pl.when(c == nsteps - 1)
    def _():
        ff_ref[...] = sf_ref[...]
        sb = h0b_ref[...]
        for cc in reversed(range(nc)):
            pb_ref[cc] = sb.astype(BF16)
            sb = sb * dball_ref[cc, 0:1, :] + sball_ref[cc]
        fb_ref[...] = sb


def _halo_specs(nsteps, col):
    per = STEP_ROWS // HALO
    cur = pl.BlockSpec((STEP_ROWS, XBC_DIM), lambda b, c: (b * nsteps + c, col))
    prev = pl.BlockSpec((HALO, XBC_DIM),
                        lambda b, c: (jnp.maximum((b * nsteps + c) * per - 1, 0), col))
    nxt = pl.BlockSpec((HALO, XBC_DIM),
                       lambda b, c: (jnp.minimum((b * nsteps + c + 1) * per,
                                                 (b * nsteps + nsteps) * per - 1), col))
    return cur, prev, nxt


def _full_spec(a):
    return pl.BlockSpec(a.shape, lambda b, c: (0,) * a.ndim)


def _scan(p, dt, h0f, h0b, consts, *, nb, nc, col):
    assert nc % CHUNKS_PER_STEP == 0
    nsteps = nc // CHUNKS_PER_STEP
    state_spec = pl.BlockSpec((None, SSD_STATE, SSD_DIM), lambda b, c: (b, 0, 0))
    step_rows = lambda width: pl.BlockSpec((STEP_ROWS, width), lambda b, c: (b * nsteps + c, 0))
    args = (p, p, p, dt, h0f, h0b) + tuple(consts)
    in_specs = list(_halo_specs(nsteps, col)) + [step_rows(LANES), state_spec, state_spec]
    in_specs += [_full_spec(a) for a in consts]
    return pl.pallas_call(
        functools.partial(_scan_kernel, nc=nc),
        grid=(nb, nsteps),
        in_specs=in_specs,
        out_specs=[step_rows(XBC_DIM),
                   pl.BlockSpec((CHUNKS_PER_STEP, SSD_STATE, SSD_DIM),
                                lambda b, c: (b * nsteps + c, 0, 0)),
                   pl.BlockSpec((nc, SSD_STATE, SSD_DIM), lambda b, c: (b, 0, 0)),
                   state_spec, state_spec],
        out_shape=[jax.ShapeDtypeStruct((nb * nc * CHUNK, XBC_DIM), F32),
                   jax.ShapeDtypeStruct((nb * nc, SSD_STATE, SSD_DIM), BF16),
                   jax.ShapeDtypeStruct((nb * nc, SSD_STATE, SSD_DIM), BF16),
                   jax.ShapeDtypeStruct((nb, SSD_STATE, SSD_DIM), F32),
                   jax.ShapeDtypeStruct((nb, SSD_STATE, SSD_DIM), F32)],
        scratch_shapes=[pltpu.VMEM((CHUNKS_PER_STEP, XBC_DIM // LANES, PAD_ROWS, LANES), F32),
                        pltpu.VMEM((SSD_STATE, SSD_DIM), F32),
                        pltpu.VMEM((nc, SSD_STATE, SSD_DIM), F32),
                        pltpu.VMEM((nc, SUBLANES, SSD_DIM), F32)],
        compiler_params=_params(("arbitrary", "arbitrary")),
        name="ssd_scan",
    )(*args)


def _mixer_kernel(u_ref, v_ref, z_ref, act_ref, dt_ref, pf_ref, pb_ref,
                  x_ref, gate_ref, wout_ref, g2_ref, shift2_ref, scale2_ref,
                  bias_ref, arow_ref, tl_ref, tu_ref, ef_ref, eb_ref,
                  dskip_ref, ng_ref, lg_ref, lb_ref, ws_ref, bs_ref,
                  o_ref, h_ref, mix_ref, stage_ref):
    @pl.when(pl.program_id(0) == 0)
    def _():
        mix_ref[...] = jnp.zeros_like(mix_ref)

    def proj_piece(n):
        def run():
            cols = slice(n * PROJ_COLS, (n + 1) * PROJ_COLS)
            f = _dot(mix_ref[...], wout_ref[:, cols])
            o_ref[:, cols] = x_ref[:, cols] + gate_ref[:, cols] * f
        return run

    pieces = [proj_piece(n) for n in range(D_MODEL // PROJ_COLS)]
    per_chunk = len(pieces) // CHUNKS_PER_STEP
    for sub in range(CHUNKS_PER_STEP):
        mine = pieces[sub * per_chunk:(sub + 1) * per_chunk]
        _mixer_chunk(sub, u_ref, v_ref, z_ref, act_ref, dt_ref, pf_ref, pb_ref, bias_ref, arow_ref,
                     tl_ref, tu_ref, ef_ref, eb_ref, dskip_ref, ng_ref, lg_ref, lb_ref, ws_ref,
                     bs_ref, stage_ref, mine)
        while mine:
            mine.pop(0)()
    h_ref[...] = _norm_mod(o_ref[...], g2_ref[...], shift2_ref[...], scale2_ref[...])
    mix_ref[...] = stage_ref[...]


def _mixer_chunk(sub, u_ref, v_ref, z_ref, act_ref, dt_ref, pf_ref, pb_ref, bias_ref, arow_ref,
                 tl_ref, tu_ref, ef_ref, eb_ref, dskip_ref, ng_ref, lg_ref, lb_ref, ws_ref, bs_ref,
                 o_ref, fillers):
    def fill():
        if fillers:
            fillers.pop(0)()

    rows = slice(sub * CHUNK, (sub + 1) * CHUNK)
    xs = act_ref[rows, 0:SSD_DIM]
    gn = SSD_GROUPS * SSD_STATE
    bm = act_ref[rows, SSD_DIM:SSD_DIM + gn].astype(BF16)
    cm = act_ref[rows, SSD_DIM + gn:XBC_DIM].astype(BF16)

    dt, cum, _ = _dt_terms(dt_ref[rows, :], bias_ref[...], arow_ref[...], tl_ref[...], tu_ref[...])
    ecum = jnp.exp(cum)
    ecum_f = _expand(ecum, ef_ref[...])
    ecum_b = _expand(ecum, eb_ref[...])
    dt_t = dt.T
    cum_t = cum.T

    lane = lax.broadcasted_iota(jnp.int32, (CHUNK, SSD_DIM), 1)
    first_head = (lane % LANES) < SSD_HEAD_DIM
    xs_even = jnp.where(first_head, xs, 0.0).astype(BF16)
    xs_odd = jnp.where(first_head, 0.0, xs).astype(BF16)

    row = lax.broadcasted_iota(jnp.int32, (CHUNK, CHUNK), 0)
    col = lax.broadcasted_iota(jnp.int32, (CHUNK, CHUNK), 1)
    causal = row >= col
    anticausal = row <= col

    def decay_matrix(h):
        lf = jnp.exp(jnp.minimum(cum[:, h:h + 1] - cum_t[h:h + 1, :], 0.0))
        lf = jnp.where(causal, lf, 0.0) * dt_t[h:h + 1, :]
        hb = SSD_HEADS + h
        lb = jnp.exp(jnp.minimum(cum[:, hb:hb + 1] - cum_t[hb:hb + 1, :], 0.0))
        lb = jnp.where(anticausal, lb, 0.0) * dt_t[hb:hb + 1, :]
        return lf + lb

    pf = pf_ref[sub]
    pb = pb_ref[sub]
    heads_per_group = SSD_HEADS // SSD_GROUPS
    y_parts = []
    for g in range(SSD_GROUPS):
        cg = cm[:, g * SSD_STATE:(g + 1) * SSD_STATE]
        bg = bm[:, g * SSD_STATE:(g + 1) * SSD_STATE]
        cb_g = lax.dot_general(cg, bg, (((1,), (1,)), ((), ())), preferred_element_type=F32)
        gs = slice(g * GROUP_DIM, (g + 1) * GROUP_DIM)
        y_off = ecum_f[:, gs] * _dot(cg, pf[:, gs]) + ecum_b[:, gs] * _dot(cg, pb[:, gs])
        diag = []
        for j in range(heads_per_group // 2):
            if j == heads_per_group // 4:
                fill()
            h0 = g * heads_per_group + 2 * j
            m_pair = jnp.concatenate([(cb_g * decay_matrix(h0)).astype(BF16),
                                      (cb_g * decay_matrix(h0 + 1)).astype(BF16)], axis=1)
            ls = slice(h0 * SSD_HEAD_DIM, (h0 + 2) * SSD_HEAD_DIM)
            rhs = jnp.concatenate([xs_even[:, ls], xs_odd[:, ls]], axis=0)
            diag.append(_dot(m_pair, rhs))
        y_parts.append(jnp.concatenate(diag, axis=1) + y_off)
    y = jnp.concatenate(y_parts, axis=1) + dskip_ref[...] * xs

    yz = y * _silu(z_ref[rows, :])
    normed = []
    for g in range(SSD_GROUPS):
        part = yz[:, g * GROUP_DIM:(g + 1) * GROUP_DIM]
        normed.append(part * lax.rsqrt(jnp.mean(part * part, axis=-1, keepdims=True) + EPS))
    ssd_out = jnp.concatenate(normed, axis=1) * ng_ref[...]
    o_ref[rows, A_DIM:A_DIM + SSD_DIM] = ssd_out.astype(BF16)

    fill()
    v = _gelu(v_ref[rows, :])
    vc = v - jnp.mean(v, axis=-1, keepdims=True)
    vn = vc * lax.rsqrt(jnp.mean(vc * vc, axis=-1, keepdims=True) + EPS) * lg_ref[...] + lb_ref[...]
    vn = vn.astype(BF16)
    for h in range(A_HEADS):
        if h == A_HEADS // 2:
            fill()
        hs = slice(h * A_HEAD_DIM, (h + 1) * A_HEAD_DIM)
        s = _dot(ws_ref[h], vn[:, hs]) + bs_ref[:, hs]
        o_ref[rows, hs] = (_gelu(u_ref[rows, hs]) * s).astype(BF16)


def _mixer(p, act, dt, prev_f, prev_b, x2d, gate, w_out, g2, shift2, scale2, consts, extra, *,
           nb, nc):
    steps_per_batch = nc // CHUNKS_PER_STEP
    n_blocks = nb * steps_per_batch
    cur = lambda t: jnp.minimum(t, n_blocks - 1)
    lag = lambda t: jnp.maximum(t - 1, 0)
    lane_blk = lambda k: pl.BlockSpec((STEP_ROWS, A_DIM), lambda t: (cur(t), k))
    cur_rows = lambda width: pl.BlockSpec((STEP_ROWS, width), lambda t: (cur(t), 0))
    lag_rows = pl.BlockSpec((STEP_ROWS, D_MODEL), lambda t: (lag(t), 0))
    lag_mod = pl.BlockSpec((None, 1, D_MODEL), lambda t: (lag(t) // steps_per_batch, 0, 0))
    state_spec = pl.BlockSpec((CHUNKS_PER_STEP, SSD_STATE, SSD_DIM), lambda t: (cur(t), 0, 0))
    whole = lambda a: pl.BlockSpec(a.shape, lambda t: (0,) * a.ndim)
    params = tuple(consts[2:]) + tuple(extra)
    args = (p, p, p, act, dt, prev_f, prev_b, x2d, gate, w_out, g2, shift2, scale2) + params
    in_specs = [lane_blk(0), lane_blk(1), lane_blk(2), cur_rows(XBC_DIM), cur_rows(LANES),
                state_spec, state_spec, lag_rows, lag_mod, whole(w_out), whole(g2), lag_mod, lag_mod]
    in_specs += [whole(a) for a in params]
    t_rows = n_blocks * STEP_ROWS
    return pl.pallas_call(
        _mixer_kernel,
        grid=(n_blocks + 1,),
        in_specs=in_specs,
        out_specs=[lag_rows, lag_rows],
        out_shape=[jax.ShapeDtypeStruct((t_rows, D_MODEL), F32),
                   jax.ShapeDtypeStruct((t_rows, D_MODEL), BF16)],
        scratch_shapes=[pltpu.VMEM((STEP_ROWS, A_DIM + SSD_DIM), BF16),
                        pltpu.VMEM((STEP_ROWS, A_DIM + SSD_DIM), BF16)],
        compiler_params=_params(("arbitrary",)),
        name="mixer",
    )(*args)


def _ffn_weights(w_gate, w_up, w_down):
    return w_gate.astype(BF16), w_up.astype(BF16), w_down.astype(BF16)


def _ssd_consts(conv_w, conv_b, dt_bias, a_log):
    lanes_pad = LANES - 2 * SSD_HEADS
    bias_row = jnp.pad(dt_bias.reshape(1, 2 * SSD_HEADS), ((0, 0), (0, lanes_pad)))
    a_row = jnp.pad(-jnp.exp(a_log.reshape(1, 2 * SSD_HEADS)), ((0, 0), (0, lanes_pad)))
    tl = jnp.asarray(np.tril(np.ones((CHUNK, CHUNK), np.float32)))
    head_of_lane = np.arange(SSD_DIM) // SSD_HEAD_DIM
    sel = np.arange(LANES)[:, None] == head_of_lane[None, :]
    sel_b = np.arange(LANES)[:, None] == (head_of_lane[None, :] + SSD_HEADS)
    ef = jnp.asarray(sel.astype(np.float32)).astype(BF16)
    eb = jnp.asarray(sel_b.astype(np.float32)).astype(BF16)
    return (conv_w, conv_b.reshape(1, XBC_DIM), bias_row, a_row, tl, tl.T, ef, eb)


def kernel(x, c, ctx, c_ctx, w_mod, b_mod, norm_ffn1, ffn1_w_gate, ffn1_w_up, ffn1_w_down,
           norm_mix, w_in, conv_w, conv_b, dt_bias, a_log, d_skip, ssd_norm_g, gmlp_norm_g,
           gmlp_norm_b, gmlp_w_s, gmlp_b_s, w_out, norm_ffn2, ffn2_w_gate, ffn2_w_up,
           ffn2_w_down, norm_final):
    assert w_mod.shape[0] == 1, "single-layer block"
    bsz, seq, _ = x.shape
    ctx_len = ctx.shape[1]
    nc_x = seq // CHUNK
    nc_c = ctx_len // CHUNK

    pad_rows = (-(bsz + 1)) % SUBLANES
    c_all = jnp.concatenate([c, c_ctx[None, :], jnp.zeros((pad_rows, D_MODEL), F32)], axis=0)
    mod = _modulation(c_all, w_mod[0], b_mod[0]).reshape(-1, N_MOD, D_MODEL)
    mod_x = [mod[:bsz, k][:, None, :] for k in range(N_MOD)]
    mod_c = [mod[bsz:bsz + 1, k][:, None, :] for k in range(N_MOD)]

    row = lambda a: a.reshape(1, -1)
    x2d = x.reshape(bsz * seq, D_MODEL)
    c2d = ctx.reshape(bsz * ctx_len, D_MODEL)

    w1 = _ffn_weights(ffn1_w_gate[0], ffn1_w_up[0], ffn1_w_down[0])
    g1 = row(norm_ffn1[0])
    x1 = _ffn(x2d, None, mod_x[0], mod_x[1], mod_x[2], g1, *w1, g1, rows_per_mod=seq,
              final_norm=False)
    c1 = _ffn(c2d, None, mod_c[0], mod_c[1], mod_c[2], g1, *w1, g1, rows_per_mod=bsz * ctx_len,
              final_norm=False)

    w_proj = jnp.pad(w_in[0].astype(BF16), ((0, 0), (0, LANES - 2 * SSD_HEADS)))
    gm = row(norm_mix[0])
    main_blocks = MAIN_COLS // XBC_DIM
    px, dtx = _inproj(x1, mod_x[3], mod_x[4], gm, w_proj, rows_per_mod=seq, first_block=0,
                      n_blocks=main_blocks)
    pc, dtc = _inproj(c1, mod_c[3], mod_c[4], gm, w_proj, rows_per_mod=bsz * ctx_len,
                      first_block=main_blocks - 1, n_blocks=1)

    consts = _ssd_consts(conv_w[0], conv_b[0], dt_bias[0], a_log[0])
    h0 = jnp.zeros((bsz, SSD_STATE, SSD_DIM), F32)
    _, _, _, hf, hb = _scan(pc, dtc, h0, h0, consts, nb=bsz, nc=nc_c, col=0)
    act, prev_f, prev_b, _, _ = _scan(px, dtx, hf, hb, consts, nb=bsz, nc=nc_x,
                                      col=(MAIN_COLS - XBC_DIM) // XBC_DIM)

    extra = (row(jnp.repeat(d_skip[0], SSD_HEAD_DIM)), row(ssd_norm_g[0]), row(gmlp_norm_g[0]),
             row(gmlp_norm_b[0]), gmlp_w_s[0].astype(BF16),
             jnp.repeat(gmlp_b_s[0].T, A_HEAD_DIM, axis=1))
    x2, h2 = _mixer(px, act, dtx, prev_f, prev_b, x1, mod_x[5], w_out[0].astype(BF16),
                    row(norm_ffn2[0]), mod_x[6], mod_x[7], consts, extra, nb=bsz, nc=nc_x)

    w2 = _ffn_weights(ffn2_w_gate[0], ffn2_w_up[0], ffn2_w_down[0])
    out = _ffn(x2, h2, None, None, mod_x[8], None, *w2, row(norm_final), rows_per_mod=seq,
               final_norm=True)
    return out.reshape(bsz, seq, D_MODEL)
```

```python
import functools
import math

import jax
import jax.numpy as jnp
import numpy as np
from jax import lax
from jax.experimental import pallas as pl
from jax.experimental.pallas import tpu as pltpu

F32 = jnp.float32
BF16 = jnp.bfloat16

D_MODEL = 2048
N_MOD = 9
D_FF = 5504
A_HEADS = 8
A_HEAD_DIM = 128
A_DIM = A_HEADS * A_HEAD_DIM
SSD_HEADS = 16
SSD_HEAD_DIM = 64
SSD_DIM = SSD_HEADS * SSD_HEAD_DIM
SSD_GROUPS = 2
SSD_STATE = 128
GROUP_DIM = SSD_DIM // SSD_GROUPS
CHUNK = 128
CONV_W = 5
XBC_DIM = SSD_DIM + 2 * SSD_GROUPS * SSD_STATE
MAIN_COLS = 2 * A_DIM + SSD_DIM + XBC_DIM
EPS = 1e-6

LANES = 128
SUBLANES = 8
VMEM_LIMIT_BYTES = 60 * 1024 * 1024

HALO = SUBLANES
PAD_ROWS = CHUNK + 2 * HALO


def _params(sem):
    return pltpu.CompilerParams(dimension_semantics=sem, vmem_limit_bytes=VMEM_LIMIT_BYTES)


def _dot(a, b):
    return jnp.dot(a, b, preferred_element_type=F32)


def _silu(x):
    return x * jax.nn.sigmoid(x)


def _gelu(x):
    return 0.5 * x * (1.0 + lax.erf(x * (1.0 / math.sqrt(2.0))))


def _softplus(x):
    return jnp.maximum(x, 0.0) + jnp.log1p(jnp.exp(-jnp.abs(x)))


def _rms(x, g):
    return x * lax.rsqrt(jnp.mean(x * x, axis=-1, keepdims=True) + EPS) * g


def _mod_kernel(c_ref, w_ref, b_ref, o_ref):
    o_ref[...] = _dot(_silu(c_ref[...]), w_ref[...]) + b_ref[...]


def _modulation(c_all, w_mod, b_mod, tn=1024):
    rows = c_all.shape[0]
    n = w_mod.shape[1]
    return pl.pallas_call(
        _mod_kernel,
        grid=(n // tn,),
        in_specs=[pl.BlockSpec((rows, D_MODEL), lambda j: (0, 0)),
                  pl.BlockSpec((D_MODEL, tn), lambda j: (0, j)),
                  pl.BlockSpec((1, tn), lambda j: (0, j))],
        out_specs=pl.BlockSpec((rows, tn), lambda j: (0, j)),
        out_shape=jax.ShapeDtypeStruct((rows, n), F32),
        compiler_params=_params(("arbitrary",)),
        name="mod",
    )(c_all, w_mod, b_mod.reshape(1, n))


FFN_TM = 1024
FFN_TF = 512


def _norm_mod(x, g, shift, scale):
    return (_rms(x, g) * (1.0 + scale) + shift).astype(BF16)


def _ffn_kernel(*refs, final_norm, last_cols, pre_normed):
    if pre_normed:
        x_ref, h_ref, gate_ref, wg_ref, wu_ref, wd_ref, gf_ref, o_ref = refs
    else:
        (x_ref, shift_ref, scale_ref, gate_ref, g_ref, wg_ref, wu_ref, wd_ref, gf_ref,
         o_ref, h_ref) = refs
    j = pl.program_id(1)
    last = pl.num_programs(1) - 1

    if not pre_normed:
        @pl.when(j == 0)
        def _():
            h_ref[...] = _norm_mod(x_ref[...], g_ref[...], shift_ref[...], scale_ref[...])

    def partial_ffn(cols):
        h = h_ref[...]
        gate = _dot(h, wg_ref[:, 0:cols])
        up = _dot(h, wu_ref[:, 0:cols])
        a = (_silu(gate) * up).astype(BF16)
        return _dot(a, wd_ref[0:cols, :])

    @pl.when(j == 0)
    def _():
        o_ref[...] = partial_ffn(FFN_TF)

    @pl.when((j > 0) & (j < last))
    def _():
        o_ref[...] += partial_ffn(FFN_TF)

    @pl.when(j == last)
    def _():
        f = o_ref[...] + partial_ffn(last_cols)
        r = x_ref[...] + (0.5 * gate_ref[...]) * f
        if final_norm:
            r = _rms(r, gf_ref[...])
        o_ref[...] = r


def _ffn(x2d, h2d, shift, scale, gate, g, wg, wu, wd, gf, *, rows_per_mod, final_norm):
    t = x2d.shape[0]
    tm, tf = min(FFN_TM, rows_per_mod), FFN_TF
    nf = pl.cdiv(D_FF, tf)
    assert nf >= 3 and t % tm == 0 and rows_per_mod % tm == 0
    pre_normed = h2d is not None
    tiles_per_mod = rows_per_mod // tm
    mod_spec = pl.BlockSpec((None, 1, D_MODEL), lambda i, j: (i // tiles_per_mod, 0, 0))
    row_spec = pl.BlockSpec((1, D_MODEL), lambda i, j: (0, 0))
    tile_spec = pl.BlockSpec((tm, D_MODEL), lambda i, j: (i, 0))
    w_specs = [pl.BlockSpec((D_MODEL, tf), lambda i, j: (0, j)),
               pl.BlockSpec((D_MODEL, tf), lambda i, j: (0, j)),
               pl.BlockSpec((tf, D_MODEL), lambda i, j: (j, 0))]
    if pre_normed:
        args = (x2d, h2d, gate, wg, wu, wd, gf)
        in_specs = [tile_spec, tile_spec, mod_spec] + w_specs + [row_spec]
        scratch = []
    else:
        args = (x2d, shift, scale, gate, g, wg, wu, wd, gf)
        in_specs = [tile_spec, mod_spec, mod_spec, mod_spec, row_spec] + w_specs + [row_spec]
        scratch = [pltpu.VMEM((tm, D_MODEL), BF16)]
    return pl.pallas_call(
        functools.partial(_ffn_kernel, final_norm=final_norm, last_cols=D_FF - (nf - 1) * tf,
                          pre_normed=pre_normed),
        grid=(t // tm, nf),
        in_specs=in_specs,
        out_specs=tile_spec,
        out_shape=jax.ShapeDtypeStruct((t, D_MODEL), F32),
        scratch_shapes=scratch,
        compiler_params=_params(("arbitrary", "arbitrary")),
        name="ffn_final" if final_norm else "ffn",
    )(*args)


def _inproj_kernel(x_ref, shift_ref, scale_ref, g_ref, w_ref, wdt_ref, p_ref, dt_ref, h_ref):
    @pl.when(pl.program_id(1) == 0)
    def _():
        h_ref[...] = _norm_mod(x_ref[...], g_ref[...], shift_ref[...], scale_ref[...])
        p_ref[...] = _dot(h_ref[...], w_ref[...])
        dt_ref[...] = _dot(h_ref[...], wdt_ref[...])

    @pl.when(pl.program_id(1) > 0)
    def _():
        p_ref[...] = _dot(h_ref[...], w_ref[...])


def _inproj(x2d, shift, scale, g, w, *, rows_per_mod, first_block, n_blocks, tn=XBC_DIM):
    t = x2d.shape[0]
    tm = min(1024, rows_per_mod)
    tiles_per_mod = rows_per_mod // tm
    mod_spec = pl.BlockSpec((None, 1, D_MODEL), lambda i, j: (i // tiles_per_mod, 0, 0))
    return pl.pallas_call(
        _inproj_kernel,
        grid=(t // tm, n_blocks),
        in_specs=[pl.BlockSpec((tm, D_MODEL), lambda i, j: (i, 0)), mod_spec, mod_spec,
                  pl.BlockSpec((1, D_MODEL), lambda i, j: (0, 0)),
                  pl.BlockSpec((D_MODEL, tn), lambda i, j: (0, first_block + j)),
                  pl.BlockSpec((D_MODEL, LANES), lambda i, j: (0, MAIN_COLS // LANES))],
        out_specs=[pl.BlockSpec((tm, tn), lambda i, j: (i, j)),
                   pl.BlockSpec((tm, LANES), lambda i, j: (i, 0))],
        out_shape=[jax.ShapeDtypeStruct((t, n_blocks * tn), F32),
                   jax.ShapeDtypeStruct((t, LANES), F32)],
        scratch_shapes=[pltpu.VMEM((tm, D_MODEL), BF16)],
        compiler_params=_params(("arbitrary", "arbitrary")),
        name="inproj",
    )(x2d, shift, scale, g, w, w)


def _conv_silu(cur_ref, r0, prev_rows, next_rows, w_ref, b_ref, pad_ref, sub):
    outs = []
    for s in range(XBC_DIM // LANES):
        ls = slice(s * LANES, (s + 1) * LANES)
        pad_ref[sub, s, 0:HALO, :] = prev_rows(ls)
        pad_ref[sub, s, HALO:HALO + CHUNK, :] = cur_ref[r0:r0 + CHUNK, ls]
        pad_ref[sub, s, HALO + CHUNK:PAD_ROWS, :] = next_rows(ls)
        acc = jnp.broadcast_to(b_ref[:, ls], (CHUNK, LANES))
        for k in range(CONV_W):
            start = HALO - CONV_W // 2 + k
            acc = acc + pad_ref[sub, s, pl.ds(start, CHUNK, stride=1), :] * w_ref[k:k + 1, ls]
        outs.append(_silu(acc))
    return jnp.concatenate(outs, axis=1)


def _dt_terms(dt_raw, bias_row, a_row, tl, tu):
    dt = _softplus(dt_raw + bias_row)
    a = dt * a_row
    s = jnp.dot(tl, a, precision=lax.Precision.HIGHEST, preferred_element_type=F32)
    r = jnp.dot(tu, a, precision=lax.Precision.HIGHEST, preferred_element_type=F32)
    is_fwd = lax.broadcasted_iota(jnp.int32, (CHUNK, LANES), 1) < SSD_HEADS
    cum = jnp.where(is_fwd, s, r)
    total = jnp.where(is_fwd[0:1], s[CHUNK - 1:CHUNK], r[0:1])
    return dt, cum, total


def _expand(v, e):
    hi = v.astype(BF16)
    lo = (v - hi.astype(F32)).astype(BF16)
    return _dot(hi, e) + _dot(lo, e)


PROJ_COLS = 256
CHUNKS_PER_STEP = 2
STEP_ROWS = CHUNKS_PER_STEP * CHUNK


def _scan_kernel(xc_ref, xp_ref, xn_ref, dt_ref, h0f_ref, h0b_ref, cw_ref, cb_ref, bias_ref,
                 arow_ref, tl_ref, tu_ref, ef_ref, eb_ref, act_ref, pf_ref, pb_ref, ff_ref, fb_ref,
                 pad_ref, sf_ref, sball_ref, dball_ref, *, nc):
    c = pl.program_id(1)
    nsteps = nc // CHUNKS_PER_STEP

    @pl.when(c == 0)
    def _():
        sf_ref[...] = h0f_ref[...]

    state = sf_ref[...]
    for sub in range(CHUNKS_PER_STEP):
        r0 = sub * CHUNK
        if sub == 0:
            prev_rows = lambda ls: jnp.where(c == 0, 0.0, xp_ref[:, ls])
        else:
            prev_rows = lambda ls, r0=r0: xc_ref[r0 - HALO:r0, ls]
        if sub == CHUNKS_PER_STEP - 1:
            next_rows = lambda ls: jnp.where(c == nsteps - 1, 0.0, xn_ref[:, ls])
        else:
            next_rows = lambda ls, r0=r0: xc_ref[r0 + CHUNK:r0 + CHUNK + HALO, ls]
        xb = _conv_silu(xc_ref, r0, prev_rows, next_rows, cw_ref, cb_ref, pad_ref, sub)
        act_ref[r0:r0 + CHUNK, :] = xb
        dt, cum, total = _dt_terms(dt_ref[r0:r0 + CHUNK, :], bias_ref[...], arow_ref[...],
                                   tl_ref[...], tu_ref[...])
        w = dt * jnp.exp(total - cum)
        decay = jnp.broadcast_to(jnp.exp(total), (SUBLANES, LANES))
        v = jnp.concatenate([w, decay], axis=0)
        xs = xb[:, 0:SSD_DIM]
        bt = []
        for g in range(SSD_GROUPS):
            lo = SSD_DIM + g * SSD_STATE
            bt.append(xb[:, lo:lo + SSD_STATE].T.astype(BF16))

        def chunk_state(e_ref):
            ex = _expand(v, e_ref[...])
            xw = (xs * ex[0:CHUNK]).astype(BF16)
            new = [_dot(bt[g], xw[:, g * GROUP_DIM:(g + 1) * GROUP_DIM])
                   for g in range(SSD_GROUPS)]
            return jnp.concatenate(new, axis=1), ex[CHUNK:CHUNK + SUBLANES]

        new_f, decay_f = chunk_state(ef_ref)
        pf_ref[sub] = state.astype(BF16)
        state = state * decay_f[0:1] + new_f

        new_b, decay_b = chunk_state(eb_ref)
        sball_ref[c * CHUNKS_PER_STEP + sub] = new_b
        dball_ref[c * CHUNKS_PER_STEP + sub] = decay_b
    sf_ref[...] = state

    @pl.when(c == nsteps - 1)
    def _():
        ff_ref[...] = sf_ref[...]
        sb = h0b_ref[...]
        for cc in reversed(range(nc)):
            pb_ref[cc] = sb.astype(BF16)
            sb = sb * dball_ref[cc, 0:1, :] + sball_ref[cc]
        fb_ref[...] = sb


def _halo_specs(nsteps, col):
    per = STEP_ROWS // HALO
    cur = pl.BlockSpec((STEP_ROWS, XBC_DIM), lambda b, c: (b * nsteps + c, col))
    prev = pl.BlockSpec((HALO, XBC_DIM),
                        lambda b, c: (jnp.maximum((b * nsteps + c) * per - 1, 0), col))
    nxt = pl.BlockSpec((HALO, XBC_DIM),
                       lambda b, c: (jnp.minimum((b * nsteps + c + 1) * per,
                                                 (b * nsteps + nsteps) * per - 1), col))
    return cur, prev, nxt


def _full_spec(a):
    return pl.BlockSpec(a.shape, lambda b, c: (0,) * a.ndim)


def _scan(p, dt, h0f, h0b, consts, *, nb, nc, col):
    assert nc % CHUNKS_PER_STEP == 0
    nsteps = nc // CHUNKS_PER_STEP
    state_spec = pl.BlockSpec((None, SSD_STATE, SSD_DIM), lambda b, c: (b, 0, 0))
    step_rows = lambda width: pl.BlockSpec((STEP_ROWS, width), lambda b, c: (b * nsteps + c, 0))
    args = (p, p, p, dt, h0f, h0b) + tuple(consts)
    in_specs = list(_halo_specs(nsteps, col)) + [step_rows(LANES), state_spec, state_spec]
    in_specs += [_full_spec(a) for a in consts]
    return pl.pallas_call(
        functools.partial(_scan_kernel, nc=nc),
        grid=(nb, nsteps),
        in_specs=in_specs,
        out_specs=[step_rows(XBC_DIM),
                   pl.BlockSpec((CHUNKS_PER_STEP, SSD_STATE, SSD_DIM),
                                lambda b, c: (b * nsteps + c, 0, 0)),
                   pl.BlockSpec((nc, SSD_STATE, SSD_DIM), lambda b, c: (b, 0, 0)),
                   state_spec, state_spec],
        out_shape=[jax.ShapeDtypeStruct((nb * nc * CHUNK, XBC_DIM), F32),
                   jax.ShapeDtypeStruct((nb * nc, SSD_STATE, SSD_DIM), BF16),
                   jax.ShapeDtypeStruct((nb * nc, SSD_STATE, SSD_DIM), BF16),
                   jax.ShapeDtypeStruct((nb, SSD_STATE, SSD_DIM), F32),
                   jax.ShapeDtypeStruct((nb, SSD_STATE, SSD_DIM), F32)],
        scratch_shapes=[pltpu.VMEM((CHUNKS_PER_STEP, XBC_DIM // LANES, PAD_ROWS, LANES), F32),
                        pltpu.VMEM((SSD_STATE, SSD_DIM), F32),
                        pltpu.VMEM((nc, SSD_STATE, SSD_DIM), F32),
                        pltpu.VMEM((nc, SUBLANES, SSD_DIM), F32)],
        compiler_params=_params(("arbitrary", "arbitrary")),
        name="ssd_scan",
    )(*args)


def _mixer_kernel(u_ref, v_ref, z_ref, act_ref, dt_ref, pf_ref, pb_ref,
                  x_ref, gate_ref, wout_ref, g2_ref, shift2_ref, scale2_ref,
                  bias_ref, arow_ref, tl_ref, tu_ref, ef_ref, eb_ref,
                  dskip_ref, ng_ref, lg_ref, lb_ref, ws_ref, bs_ref,
                  o_ref, h_ref, mix_ref, stage_ref):
    @pl.when(pl.program_id(0) == 0)
    def _():
        mix_ref[...] = jnp.zeros_like(mix_ref)

    def proj_piece(n):
        def run():
            cols = slice(n * PROJ_COLS, (n + 1) * PROJ_COLS)
            f = _dot(mix_ref[...], wout_ref[:, cols])
            o_ref[:, cols] = x_ref[:, cols] + gate_ref[:, cols] * f
        return run

    pieces = [proj_piece(n) for n in range(D_MODEL // PROJ_COLS)]
    per_chunk = len(pieces) // CHUNKS_PER_STEP
    for sub in range(CHUNKS_PER_STEP):
        mine = pieces[sub * per_chunk:(sub + 1) * per_chunk]
        _mixer_chunk(sub, u_ref, v_ref, z_ref, act_ref, dt_ref, pf_ref, pb_ref, bias_ref, arow_ref,
                     tl_ref, tu_ref, ef_ref, eb_ref, dskip_ref, ng_ref, lg_ref, lb_ref, ws_ref,
                     bs_ref, stage_ref, mine)
        while mine:
            mine.pop(0)()
    h_ref[...] = _norm_mod(o_ref[...], g2_ref[...], shift2_ref[...], scale2_ref[...])
    mix_ref[...] = stage_ref[...]


def _mixer_chunk(sub, u_ref, v_ref, z_ref, act_ref, dt_ref, pf_ref, pb_ref, bias_ref, arow_ref,
                 tl_ref, tu_ref, ef_ref, eb_ref, dskip_ref, ng_ref, lg_ref, lb_ref, ws_ref, bs_ref,
                 o_ref, fillers):
    def fill():
        if fillers:
            fillers.pop(0)()

    rows = slice(sub * CHUNK, (sub + 1) * CHUNK)
    xs = act_ref[rows, 0:SSD_DIM]
    gn = SSD_GROUPS * SSD_STATE
    bm = act_ref[rows, SSD_DIM:SSD_DIM + gn].astype(BF16)
    cm = act_ref[rows, SSD_DIM + gn:XBC_DIM].astype(BF16)

    dt, cum, _ = _dt_terms(dt_ref[rows, :], bias_ref[...], arow_ref[...], tl_ref[...], tu_ref[...])
    ecum = jnp.exp(cum)
    ecum_f = _expand(ecum, ef_ref[...])
    ecum_b = _expand(ecum, eb_ref[...])
    dt_t = dt.T
    cum_t = cum.T

    lane = lax.broadcasted_iota(jnp.int32, (CHUNK, SSD_DIM), 1)
    first_head = (lane % LANES) < SSD_HEAD_DIM
    xs_even = jnp.where(first_head, xs, 0.0).astype(BF16)
    xs_odd = jnp.where(first_head, 0.0, xs).astype(BF16)

    row = lax.broadcasted_iota(jnp.int32, (CHUNK, CHUNK), 0)
    col = lax.broadcasted_iota(jnp.int32, (CHUNK, CHUNK), 1)
    causal = row >= col
    anticausal = row <= col

    def decay_matrix(h):
        lf = jnp.exp(jnp.minimum(cum[:, h:h + 1] - cum_t[h:h + 1, :], 0.0))
        lf = jnp.where(causal, lf, 0.0) * dt_t[h:h + 1, :]
        hb = SSD_HEADS + h
        lb = jnp.exp(jnp.minimum(cum[:, hb:hb + 1] - cum_t[hb:hb + 1, :], 0.0))
        lb = jnp.where(anticausal, lb, 0.0) * dt_t[hb:hb + 1, :]
        return lf + lb

    pf = pf_ref[sub]
    pb = pb_ref[sub]
    heads_per_group = SSD_HEADS // SSD_GROUPS
    y_parts = []
    for g in range(SSD_GROUPS):
        cg = cm[:, g * SSD_STATE:(g + 1) * SSD_STATE]
        bg = bm[:, g * SSD_STATE:(g + 1) * SSD_STATE]
        cb_g = lax.dot_general(cg, bg, (((1,), (1,)), ((), ())), preferred_element_type=F32)
        gs = slice(g * GROUP_DIM, (g + 1) * GROUP_DIM)
        y_off = ecum_f[:, gs] * _dot(cg, pf[:, gs]) + ecum_b[:, gs] * _dot(cg, pb[:, gs])
        diag = []
        for j in range(heads_per_group // 2):
            if j == heads_per_group // 4:
                fill()
            h0 = g * heads_per_group + 2 * j
            m_pair = jnp.concatenate([(cb_g * decay_matrix(h0)).astype(BF16),
                                      (cb_g * decay_matrix(h0 + 1)).astype(BF16)], axis=1)
            ls = slice(h0 * SSD_HEAD_DIM, (h0 + 2) * SSD_HEAD_DIM)
            rhs = jnp.concatenate([xs_even[:, ls], xs_odd[:, ls]], axis=0)
            diag.append(_dot(m_pair, rhs))
        y_parts.append(jnp.concatenate(diag, axis=1) + y_off)
    y = jnp.concatenate(y_parts, axis=1) + dskip_ref[...] * xs

    yz = y * _silu(z_ref[rows, :])
    normed = []
    for g in range(SSD_GROUPS):
        part = yz[:, g * GROUP_DIM:(g + 1) * GROUP_DIM]
        normed.append(part * lax.rsqrt(jnp.mean(part * part, axis=-1, keepdims=True) + EPS))
    ssd_out = jnp.concatenate(normed, axis=1) * ng_ref[...]
    o_ref[rows, A_DIM:A_DIM + SSD_DIM] = ssd_out.astype(BF16)

    fill()
    v = _gelu(v_ref[rows, :])
    vc = v - jnp.mean(v, axis=-1, keepdims=True)
    vn = vc * lax.rsqrt(jnp.mean(vc * vc, axis=-1, keepdims=True) + EPS) * lg_ref[...] + lb_ref[...]
    vn = vn.astype(BF16)
    for h in range(A_HEADS):
        if h == A_HEADS // 2:
            fill()
        hs = slice(h * A_HEAD_DIM, (h + 1) * A_HEAD_DIM)
        s = _dot(ws_ref[h], vn[:, hs]) + bs_ref[:, hs]
        o_ref[rows, hs] = (_gelu(u_ref[rows, hs]) * s).astype(BF16)


def _mixer(p, act, dt, prev_f, prev_b, x2d, gate, w_out, g2, shift2, scale2, consts, extra, *,
           nb, nc):
    steps_per_batch = nc // CHUNKS_PER_STEP
    n_blocks = nb * steps_per_batch
    cur = lambda t: jnp.minimum(t, n_blocks - 1)
    lag = lambda t: jnp.maximum(t - 1, 0)
    lane_blk = lambda k: pl.BlockSpec((STEP_ROWS, A_DIM), lambda t: (cur(t), k))
    cur_rows = lambda width: pl.BlockSpec((STEP_ROWS, width), lambda t: (cur(t), 0))
    lag_rows = pl.BlockSpec((STEP_ROWS, D_MODEL), lambda t: (lag(t), 0))
    lag_mod = pl.BlockSpec((None, 1, D_MODEL), lambda t: (lag(t) // steps_per_batch, 0, 0))
    state_spec = pl.BlockSpec((CHUNKS_PER_STEP, SSD_STATE, SSD_DIM), lambda t: (cur(t), 0, 0))
    whole = lambda a: pl.BlockSpec(a.shape, lambda t: (0,) * a.ndim)
    params = tuple(consts[2:]) + tuple(extra)
    args = (p, p, p, act, dt, prev_f, prev_b, x2d, gate, w_out, g2, shift2, scale2) + params
    in_specs = [lane_blk(0), lane_blk(1), lane_blk(2), cur_rows(XBC_DIM), cur_rows(LANES),
                state_spec, state_spec, lag_rows, lag_mod, whole(w_out), whole(g2), lag_mod, lag_mod]
    in_specs += [whole(a) for a in params]
    t_rows = n_blocks * STEP_ROWS
    return pl.pallas_call(
        _mixer_kernel,
        grid=(n_blocks + 1,),
        in_specs=in_specs,
        out_specs=[lag_rows, lag_rows],
        out_shape=[jax.ShapeDtypeStruct((t_rows, D_MODEL), F32),
                   jax.ShapeDtypeStruct((t_rows, D_MODEL), BF16)],
        scratch_shapes=[pltpu.VMEM((STEP_ROWS, A_DIM + SSD_DIM), BF16),
                        pltpu.VMEM((STEP_ROWS, A_DIM + SSD_DIM), BF16)],
        compiler_params=_params(("arbitrary",)),
        name="mixer",
    )(*args)


def _ffn_weights(w_gate, w_up, w_down):
    return w_gate.astype(BF16), w_up.astype(BF16), w_down.astype(BF16)


def _ssd_consts(conv_w, conv_b, dt_bias, a_log):
    lanes_pad = LANES - 2 * SSD_HEADS
    bias_row = jnp.pad(dt_bias.reshape(1, 2 * SSD_HEADS), ((0, 0), (0, lanes_pad)))
    a_row = jnp.pad(-jnp.exp(a_log.reshape(1, 2 * SSD_HEADS)), ((0, 0), (0, lanes_pad)))
    tl = jnp.asarray(np.tril(np.ones((CHUNK, CHUNK), np.float32)))
    head_of_lane = np.arange(SSD_DIM) // SSD_HEAD_DIM
    sel = np.arange(LANES)[:, None] == head_of_lane[None, :]
    sel_b = np.arange(LANES)[:, None] == (head_of_lane[None, :] + SSD_HEADS)
    ef = jnp.asarray(sel.astype(np.float32)).astype(BF16)
    eb = jnp.asarray(sel_b.astype(np.float32)).astype(BF16)
    return (conv_w, conv_b.reshape(1, XBC_DIM), bias_row, a_row, tl, tl.T, ef, eb)


def kernel(x, c, ctx, c_ctx, w_mod, b_mod, norm_ffn1, ffn1_w_gate, ffn1_w_up, ffn1_w_down,
           norm_mix, w_in, conv_w, conv_b, dt_bias, a_log, d_skip, ssd_norm_g, gmlp_norm_g,
           gmlp_norm_b, gmlp_w_s, gmlp_b_s, w_out, norm_ffn2, ffn2_w_gate, ffn2_w_up,
           ffn2_w_down, norm_final):
    assert w_mod.shape[0] == 1, "single-layer block"
    bsz, seq, _ = x.shape
    ctx_len = ctx.shape[1]
    nc_x = seq // CHUNK
    nc_c = ctx_len // CHUNK

    pad_rows = (-(bsz + 1)) % SUBLANES
    c_all = jnp.concatenate([c, c_ctx[None, :], jnp.zeros((pad_rows, D_MODEL), F32)], axis=0)
    mod = _modulation(c_all, w_mod[0], b_mod[0]).reshape(-1, N_MOD, D_MODEL)
    mod_x = [mod[:bsz, k][:, None, :] for k in range(N_MOD)]
    mod_c = [mod[bsz:bsz + 1, k][:, None, :] for k in range(N_MOD)]

    row = lambda a: a.reshape(1, -1)
    x2d = x.reshape(bsz * seq, D_MODEL)
    c2d = ctx.reshape(bsz * ctx_len, D_MODEL)

    w1 = _ffn_weights(ffn1_w_gate[0], ffn1_w_up[0], ffn1_w_down[0])
    g1 = row(norm_ffn1[0])
    x1 = _ffn(x2d, None, mod_x[0], mod_x[1], mod_x[2], g1, *w1, g1, rows_per_mod=seq,
              final_norm=False)
    c1 = _ffn(c2d, None, mod_c[0], mod_c[1], mod_c[2], g1, *w1, g1, rows_per_mod=bsz * ctx_len,
              final_norm=False)

    w_proj = jnp.pad(w_in[0].astype(BF16), ((0, 0), (0, LANES - 2 * SSD_HEADS)))
    gm = row(norm_mix[0])
    main_blocks = MAIN_COLS // XBC_DIM
    px, dtx = _inproj(x1, mod_x[3], mod_x[4], gm, w_proj, rows_per_mod=seq, first_block=0,
                      n_blocks=main_blocks)
    pc, dtc = _inproj(c1, mod_c[3], mod_c[4], gm, w_proj, rows_per_mod=bsz * ctx_len,
                      first_block=main_blocks - 1, n_blocks=1)

    consts = _ssd_consts(conv_w[0], conv_b[0], dt_bias[0], a_log[0])
    h0 = jnp.zeros((bsz, SSD_STATE, SSD_DIM), F32)
    _, _, _, hf, hb = _scan(pc, dtc, h0, h0, consts, nb=bsz, nc=nc_c, col=0)
    act, prev_f, prev_b, _, _ = _scan(px, dtx, hf, hb, consts, nb=bsz, nc=nc_x,
                                      col=(MAIN_COLS - XBC_DIM) // XBC_DIM)

    extra = (row(jnp.repeat(d_skip[0], SSD_HEAD_DIM)), row(ssd_norm_g[0]), row(gmlp_norm_g[0]),
             row(gmlp_norm_b[0]), gmlp_w_s[0].astype(BF16),
             jnp.repeat(gmlp_b_s[0].T, A_HEAD_DIM, axis=1))
    x2, h2 = _mixer(px, act, dtx, prev_f, prev_b, x1, mod_x[5], w_out[0].astype(BF16),
                    row(norm_ffn2[0]), mod_x[6], mod_x[7], consts, extra, nb=bsz, nc=nc_x)

    w2 = _ffn_weights(ffn2_w_gate[0], ffn2_w_up[0], ffn2_w_down[0])
    out = _ffn(x2, h2, None, None, mod_x[8], None, *w2, row(norm_final), rows_per_mod=seq,
               final_norm=True)
    return out.reshape(bsz, seq, D_MODEL)
```

```python
import functools
import math

import jax
import jax.numpy as jnp
import numpy as np
from jax import lax
from jax.experimental import pallas as pl
from jax.experimental.pallas import tpu as pltpu

F32 = jnp.float32
BF16 = jnp.bfloat16

D_MODEL = 2048
N_MOD = 9
D_FF = 5504
A_HEADS = 8
A_HEAD_DIM = 128
A_DIM = A_HEADS * A_HEAD_DIM
SSD_HEADS = 16
SSD_HEAD_DIM = 64
SSD_DIM = SSD_HEADS * SSD_HEAD_DIM
SSD_GROUPS = 2
SSD_STATE = 128
GROUP_DIM = SSD_DIM // SSD_GROUPS
CHUNK = 128
CONV_W = 5
XBC_DIM = SSD_DIM + 2 * SSD_GROUPS * SSD_STATE
MAIN_COLS = 2 * A_DIM + SSD_DIM + XBC_DIM
EPS = 1e-6

LANES = 128
SUBLANES = 8
VMEM_LIMIT_BYTES = 60 * 1024 * 1024

HALO = SUBLANES
PAD_ROWS = CHUNK + 2 * HALO


def _params(sem):
    return pltpu.CompilerParams(dimension_semantics=sem, vmem_limit_bytes=VMEM_LIMIT_BYTES)


def _dot(a, b):
    return jnp.dot(a, b, preferred_element_type=F32)


def _silu(x):
    return x * jax.nn.sigmoid(x)


def _gelu(x):
    return 0.5 * x * (1.0 + lax.erf(x * (1.0 / math.sqrt(2.0))))


def _softplus(x):
    return jnp.maximum(x, 0.0) + jnp.log1p(jnp.exp(-jnp.abs(x)))


def _rms(x, g):
    return x * lax.rsqrt(jnp.mean(x * x, axis=-1, keepdims=True) + EPS) * g


def _mod_kernel(c_ref, w_ref, b_ref, o_ref):
    o_ref[...] = _dot(_silu(c_ref[...]), w_ref[...]) + b_ref[...]


def _modulation(c_all, w_mod, b_mod, tn=1024):
    rows = c_all.shape[0]
    n = w_mod.shape[1]
    return pl.pallas_call(
        _mod_kernel,
        grid=(n // tn,),
        in_specs=[pl.BlockSpec((rows, D_MODEL), lambda j: (0, 0)),
                  pl.BlockSpec((D_MODEL, tn), lambda j: (0, j)),
                  pl.BlockSpec((1, tn), lambda j: (0, j))],
        out_specs=pl.BlockSpec((rows, tn), lambda j: (0, j)),
        out_shape=jax.ShapeDtypeStruct((rows, n), F32),
        compiler_params=_params(("arbitrary",)),
        name="mod",
    )(c_all, w_mod, b_mod.reshape(1, n))


FFN_TM = 1024
FFN_TF = 512


def _norm_mod(x, g, shift, scale):
    return (_rms(x, g) * (1.0 + scale) + shift).astype(BF16)


def _ffn_kernel(x_ref, shift_ref, scale_ref, gate_ref, g_ref, wg_ref, wu_ref, wd_ref, gf_ref,
                o_ref, h_ref, *, final_norm, last_cols):
    j = pl.program_id(1)
    last = pl.num_programs(1) - 1

    @pl.when(j == 0)
    def _():
        h_ref[...] = _norm_mod(x_ref[...], g_ref[...], shift_ref[...], scale_ref[...])

    def partial_ffn(cols):
        h = h_ref[...]
        gate = _dot(h, wg_ref[:, 0:cols])
        up = _dot(h, wu_ref[:, 0:cols])
        a = (_silu(gate) * up).astype(BF16)
        return _dot(a, wd_ref[0:cols, :])

    @pl.when(j == 0)
    def _():
        o_ref[...] = partial_ffn(FFN_TF)

    @pl.when((j > 0) & (j < last))
    def _():
        o_ref[...] += partial_ffn(FFN_TF)

    @pl.when(j == last)
    def _():
        f = o_ref[...] + partial_ffn(last_cols)
        r = x_ref[...] + (0.5 * gate_ref[...]) * f
        if final_norm:
            r = _rms(r, gf_ref[...])
        o_ref[...] = r


def _ffn(x2d, shift, scale, gate, g, wg, wu, wd, gf, *, rows_per_mod, final_norm):
    t = x2d.shape[0]
    tm, tf = min(FFN_TM, rows_per_mod), FFN_TF
    nf = pl.cdiv(D_FF, tf)
    assert nf >= 3 and t % tm == 0 and rows_per_mod % tm == 0
    tiles_per_mod = rows_per_mod // tm
    mod_spec = pl.BlockSpec((None, 1, D_MODEL), lambda i, j: (i // tiles_per_mod, 0, 0))
    row_spec = pl.BlockSpec((1, D_MODEL), lambda i, j: (0, 0))
    tile_spec = pl.BlockSpec((tm, D_MODEL), lambda i, j: (i, 0))
    w_specs = [pl.BlockSpec((D_MODEL, tf), lambda i, j: (0, j)),
               pl.BlockSpec((D_MODEL, tf), lambda i, j: (0, j)),
               pl.BlockSpec((tf, D_MODEL), lambda i, j: (j, 0))]
    args = (x2d, shift, scale, gate, g, wg, wu, wd, gf)
    in_specs = [tile_spec, mod_spec, mod_spec, mod_spec, row_spec] + w_specs + [row_spec]
    return pl.pallas_call(
        functools.partial(_ffn_kernel, final_norm=final_norm, last_cols=D_FF - (nf - 1) * tf),
        grid=(t // tm, nf),
        in_specs=in_specs,
        out_specs=tile_spec,
        out_shape=jax.ShapeDtypeStruct((t, D_MODEL), F32),
        scratch_shapes=[pltpu.VMEM((tm, D_MODEL), BF16)],
        compiler_params=_params(("arbitrary", "arbitrary")),
        name="ffn_final" if final_norm else "ffn",
    )(*args)


def _inproj_kernel(x_ref, shift_ref, scale_ref, g_ref, w_ref, wdt_ref, p_ref, dt_ref, h_ref):
    @pl.when(pl.program_id(1) == 0)
    def _():
        h_ref[...] = _norm_mod(x_ref[...], g_ref[...], shift_ref[...], scale_ref[...])
        p_ref[...] = _dot(h_ref[...], w_ref[...])
        dt_ref[...] = _dot(h_ref[...], wdt_ref[...])

    @pl.when(pl.program_id(1) > 0)
    def _():
        p_ref[...] = _dot(h_ref[...], w_ref[...])


def _inproj(x2d, shift, scale, g, w, w_dt, *, rows_per_mod, first_block, n_blocks, tn=XBC_DIM):
    t = x2d.shape[0]
    tm = min(1024, rows_per_mod)
    tiles_per_mod = rows_per_mod // tm
    mod_spec = pl.BlockSpec((None, 1, D_MODEL), lambda i, j: (i // tiles_per_mod, 0, 0))
    return pl.pallas_call(
        _inproj_kernel,
        grid=(t // tm, n_blocks),
        in_specs=[pl.BlockSpec((tm, D_MODEL), lambda i, j: (i, 0)), mod_spec, mod_spec,
                  pl.BlockSpec((1, D_MODEL), lambda i, j: (0, 0)),
                  pl.BlockSpec((D_MODEL, tn), lambda i, j: (0, first_block + j)),
                  pl.BlockSpec((D_MODEL, LANES), lambda i, j: (0, 0))],
        out_specs=[pl.BlockSpec((tm, tn), lambda i, j: (i, j)),
                   pl.BlockSpec((tm, LANES), lambda i, j: (i, 0))],
        out_shape=[jax.ShapeDtypeStruct((t, n_blocks * tn), F32),
                   jax.ShapeDtypeStruct((t, LANES), F32)],
        scratch_shapes=[pltpu.VMEM((tm, D_MODEL), BF16)],
        compiler_params=_params(("arbitrary", "arbitrary")),
        name="inproj",
    )(x2d, shift, scale, g, w, w_dt)


def _conv_silu(cur_ref, r0, prev_rows, next_rows, w_ref, b_ref, pad_ref, sub):
    outs = []
    for s in range(XBC_DIM // LANES):
        ls = slice(s * LANES, (s + 1) * LANES)
        pad_ref[sub, s, 0:HALO, :] = prev_rows(ls)
        pad_ref[sub, s, HALO:HALO + CHUNK, :] = cur_ref[r0:r0 + CHUNK, ls]
        pad_ref[sub, s, HALO + CHUNK:PAD_ROWS, :] = next_rows(ls)
        acc = jnp.broadcast_to(b_ref[:, ls], (CHUNK, LANES))
        for k in range(CONV_W):
            start = HALO - CONV_W // 2 + k
            acc = acc + pad_ref[sub, s, pl.ds(start, CHUNK, stride=1), :] * w_ref[k:k + 1, ls]
        outs.append(_silu(acc))
    return jnp.concatenate(outs, axis=1)


def _dt_terms(dt_raw, bias_row, a_row, tl, tu):
    dt = _softplus(dt_raw + bias_row)
    a = dt * a_row
    s = jnp.dot(tl, a, precision=lax.Precision.HIGHEST, preferred_element_type=F32)
    r = jnp.dot(tu, a, precision=lax.Precision.HIGHEST, preferred_element_type=F32)
    is_fwd = lax.broadcasted_iota(jnp.int32, (CHUNK, LANES), 1) < SSD_HEADS
    cum = jnp.where(is_fwd, s, r)
    total = jnp.where(is_fwd[0:1], s[CHUNK - 1:CHUNK], r[0:1])
    return dt, cum, total


def _expand(v, e):
    hi = v.astype(BF16)
    lo = (v - hi.astype(F32)).astype(BF16)
    return _dot(hi, e) + _dot(lo, e)


PROJ_COLS = 256
CHUNKS_PER_STEP = 2
STEP_ROWS = CHUNKS_PER_STEP * CHUNK


def _scan_kernel(xc_ref, xp_ref, xn_ref, dt_ref, h0f_ref, h0b_ref, cw_ref, cb_ref, bias_ref,
                 arow_ref, tl_ref, tu_ref, ef_ref, eb_ref, act_ref, pf_ref, pb_ref, ff_ref, fb_ref,
                 pad_ref, sf_ref, sball_ref, dball_ref, *, nc):
    c = pl.program_id(1)
    nsteps = nc // CHUNKS_PER_STEP

    @pl.when(c == 0)
    def _():
        sf_ref[...] = h0f_ref[...]

    state = sf_ref[...]
    for sub in range(CHUNKS_PER_STEP):
        r0 = sub * CHUNK
        if sub == 0:
            prev_rows = lambda ls: jnp.where(c == 0, 0.0, xp_ref[:, ls])
        else:
            prev_rows = lambda ls, r0=r0: xc_ref[r0 - HALO:r0, ls]
        if sub == CHUNKS_PER_STEP - 1:
            next_rows = lambda ls: jnp.where(c == nsteps - 1, 0.0, xn_ref[:, ls])
        else:
            next_rows = lambda ls, r0=r0: xc_ref[r0 + CHUNK:r0 + CHUNK + HALO, ls]
        xb = _conv_silu(xc_ref, r0, prev_rows, next_rows, cw_ref, cb_ref, pad_ref, sub)
        act_ref[r0:r0 + CHUNK, :] = xb
        dt, cum, total = _dt_terms(dt_ref[r0:r0 + CHUNK, :], bias_ref[...], arow_ref[...],
                                   tl_ref[...], tu_ref[...])
        w = dt * jnp.exp(total - cum)
        decay = jnp.broadcast_to(jnp.exp(total), (SUBLANES, LANES))
        v = jnp.concatenate([w, decay], axis=0)
        xs = xb[:, 0:SSD_DIM]
        bt = []
        for g in range(SSD_GROUPS):
            lo = SSD_DIM + g * SSD_STATE
            bt.append(xb[:, lo:lo + SSD_STATE].T.astype(BF16))

        def chunk_state(e_ref):
            ex = _expand(v, e_ref[...])
            xw = (xs * ex[0:CHUNK]).astype(BF16)
            new = [_dot(bt[g], xw[:, g * GROUP_DIM:(g + 1) * GROUP_DIM])
                   for g in range(SSD_GROUPS)]
            return jnp.concatenate(new, axis=1), ex[CHUNK:CHUNK + SUBLANES]

        new_f, decay_f = chunk_state(ef_ref)
        pf_ref[sub] = state.astype(BF16)
        state = state * decay_f[0:1] + new_f

        new_b, decay_b = chunk_state(eb_ref)
        sball_ref[c * CHUNKS_PER_STEP + sub] = new_b
        dball_ref[c * CHUNKS_PER_STEP + sub] = decay_b
    sf_ref[...] = state

    @pl.when(c == nsteps - 1)
    def _():
        ff_ref[...] = sf_ref[...]
        sb = h0b_ref[...]
        for cc in reversed(range(nc)):
            pb_ref[cc] = sb.astype(BF16)
            sb = sb * dball_ref[cc, 0:1, :] + sball_ref[cc]
        fb_ref[...] = sb


def _halo_specs(nsteps, col):
    per = STEP_ROWS // HALO
    cur = pl.BlockSpec((STEP_ROWS, XBC_DIM), lambda b, c: (b * nsteps + c, col))
    prev = pl.BlockSpec((HALO, XBC_DIM),
                        lambda b, c: (jnp.maximum((b * nsteps + c) * per - 1, 0), col))
    nxt = pl.BlockSpec((HALO, XBC_DIM),
                       lambda b, c: (jnp.minimum((b * nsteps + c + 1) * per,
                                                 (b * nsteps + nsteps) * per - 1), col))
    return cur, prev, nxt


def _full_spec(a):
    return pl.BlockSpec(a.shape, lambda b, c: (0,) * a.ndim)


def _scan(p, dt, h0f, h0b, consts, *, nb, nc, col):
    assert nc % CHUNKS_PER_STEP == 0
    nsteps = nc // CHUNKS_PER_STEP
    state_spec = pl.BlockSpec((None, SSD_STATE, SSD_DIM), lambda b, c: (b, 0, 0))
    step_rows = lambda width: pl.BlockSpec((STEP_ROWS, width), lambda b, c: (b * nsteps + c, 0))
    args = (p, p, p, dt, h0f, h0b) + tuple(consts)
    in_specs = list(_halo_specs(nsteps, col)) + [step_rows(LANES), state_spec, state_spec]
    in_specs += [_full_spec(a) for a in consts]
    return pl.pallas_call(
        functools.partial(_scan_kernel, nc=nc),
        grid=(nb, nsteps),
        in_specs=in_specs,
        out_specs=[step_rows(XBC_DIM),
                   pl.BlockSpec((CHUNKS_PER_STEP, SSD_STATE, SSD_DIM),
                                lambda b, c: (b * nsteps + c, 0, 0)),
                   pl.BlockSpec((nc, SSD_STATE, SSD_DIM), lambda b, c: (b, 0, 0)),
                   state_spec, state_spec],
        out_shape=[jax.ShapeDtypeStruct((nb * nc * CHUNK, XBC_DIM), F32),
                   jax.ShapeDtypeStruct((nb * nc, SSD_STATE, SSD_DIM), BF16),
                   jax.ShapeDtypeStruct((nb * nc, SSD_STATE, SSD_DIM), BF16),
                   jax.ShapeDtypeStruct((nb, SSD_STATE, SSD_DIM), F32),
                   jax.ShapeDtypeStruct((nb, SSD_STATE, SSD_DIM), F32)],
        scratch_shapes=[pltpu.VMEM((CHUNKS_PER_STEP, XBC_DIM // LANES, PAD_ROWS, LANES), F32),
                        pltpu.VMEM((SSD_STATE, SSD_DIM), F32),
                        pltpu.VMEM((nc, SSD_STATE, SSD_DIM), F32),
                        pltpu.VMEM((nc, SUBLANES, SSD_DIM), F32)],
        compiler_params=_params(("arbitrary", "arbitrary")),
        name="ssd_scan",
    )(*args)


def _mixer_kernel(u_ref, v_ref, z_ref, act_ref, dt_ref, pf_ref, pb_ref,
                  x_ref, gate_ref, wout_ref,
                  bias_ref, arow_ref, tl_ref, tu_ref, ef_ref, eb_ref,
                  dskip_ref, ng_ref, lg_ref, lb_ref, ws_ref, bs_ref,
                  o_ref, mix_ref, stage_ref):
    @pl.when(pl.program_id(0) == 0)
    def _():
        mix_ref[...] = jnp.zeros_like(mix_ref)

    def proj_piece(n):
        def run():
            cols = slice(n * PROJ_COLS, (n + 1) * PROJ_COLS)
            f = _dot(mix_ref[...], wout_ref[:, cols])
            o_ref[:, cols] = x_ref[:, cols] + gate_ref[:, cols] * f
        return run

    pieces = [proj_piece(n) for n in range(D_MODEL // PROJ_COLS)]
    per_chunk = len(pieces) // CHUNKS_PER_STEP
    for sub in range(CHUNKS_PER_STEP):
        mine = pieces[sub * per_chunk:(sub + 1) * per_chunk]
        _mixer_chunk(sub, u_ref, v_ref, z_ref, act_ref, dt_ref, pf_ref, pb_ref, bias_ref, arow_ref,
                     tl_ref, tu_ref, ef_ref, eb_ref, dskip_ref, ng_ref, lg_ref, lb_ref, ws_ref,
                     bs_ref, stage_ref, mine)
        while mine:
            mine.pop(0)()
    mix_ref[...] = stage_ref[...]


def _mixer_chunk(sub, u_ref, v_ref, z_ref, act_ref, dt_ref, pf_ref, pb_ref, bias_ref, arow_ref,
                 tl_ref, tu_ref, ef_ref, eb_ref, dskip_ref, ng_ref, lg_ref, lb_ref, ws_ref, bs_ref,
                 o_ref, fillers):
    def fill():
        if fillers:
            fillers.pop(0)()

    rows = slice(sub * CHUNK, (sub + 1) * CHUNK)
    xs = act_ref[rows, 0:SSD_DIM]
    gn = SSD_GROUPS * SSD_STATE
    bm = act_ref[rows, SSD_DIM:SSD_DIM + gn].astype(BF16)
    cm = act_ref[rows, SSD_DIM + gn:XBC_DIM].astype(BF16)

    dt, cum, _ = _dt_terms(dt_ref[rows, :], bias_ref[...], arow_ref[...], tl_ref[...], tu_ref[...])
    ecum = jnp.exp(cum)
    ecum_f = _expand(ecum, ef_ref[...])
    ecum_b = _expand(ecum, eb_ref[...])
    dt_t = dt.T
    cum_t = cum.T

    lane = lax.broadcasted_iota(jnp.int32, (CHUNK, SSD_DIM), 1)
    first_head = (lane % LANES) < SSD_HEAD_DIM
    xs_even = jnp.where(first_head, xs, 0.0).astype(BF16)
    xs_odd = jnp.where(first_head, 0.0, xs).astype(BF16)

    row = lax.broadcasted_iota(jnp.int32, (CHUNK, CHUNK), 0)
    col = lax.broadcasted_iota(jnp.int32, (CHUNK, CHUNK), 1)
    causal = row >= col
    anticausal = row <= col

    def decay_matrix(h):
        lf = jnp.exp(jnp.minimum(cum[:, h:h + 1] - cum_t[h:h + 1, :], 0.0))
        lf = jnp.where(causal, lf, 0.0) * dt_t[h:h + 1, :]
        hb = SSD_HEADS + h
        lb = jnp.exp(jnp.minimum(cum[:, hb:hb + 1] - cum_t[hb:hb + 1, :], 0.0))
        lb = jnp.where(anticausal, lb, 0.0) * dt_t[hb:hb + 1, :]
        return lf + lb

    pf = pf_ref[sub]
    pb = pb_ref[sub]
    heads_per_group = SSD_HEADS // SSD_GROUPS
    y_parts = []
    for g in range(SSD_GROUPS):
        cg = cm[:, g * SSD_STATE:(g + 1) * SSD_STATE]
        bg = bm[:, g * SSD_STATE:(g + 1) * SSD_STATE]
        cb_g = lax.dot_general(cg, bg, (((1,), (1,)), ((), ())), preferred_element_type=F32)
        gs = slice(g * GROUP_DIM, (g + 1) * GROUP_DIM)
        y_off = ecum_f[:, gs] * _dot(cg, pf[:, gs]) + ecum_b[:, gs] * _dot(cg, pb[:, gs])
        diag = []
        for j in range(heads_per_group // 2):
            if j == heads_per_group // 4:
                fill()
            h0 = g * heads_per_group + 2 * j
            m_pair = jnp.concatenate([(cb_g * decay_matrix(h0)).astype(BF16),
                                      (cb_g * decay_matrix(h0 + 1)).astype(BF16)], axis=1)
            ls = slice(h0 * SSD_HEAD_DIM, (h0 + 2) * SSD_HEAD_DIM)
            rhs = jnp.concatenate([xs_even[:, ls], xs_odd[:, ls]], axis=0)
            diag.append(_dot(m_pair, rhs))
        y_parts.append(jnp.concatenate(diag, axis=1) + y_off)
    y = jnp.concatenate(y_parts, axis=1) + dskip_ref[...] * xs

    yz = y * _silu(z_ref[rows, :])
    normed = []
    for g in range(SSD_GROUPS):
        part = yz[:, g * GROUP_DIM:(g + 1) * GROUP_DIM]
        normed.append(part * lax.rsqrt(jnp.mean(part * part, axis=-1, keepdims=True) + EPS))
    ssd_out = jnp.concatenate(normed, axis=1) * ng_ref[...]
    o_ref[rows, A_DIM:A_DIM + SSD_DIM] = ssd_out.astype(BF16)

    fill()
    v = _gelu(v_ref[rows, :])
    vc = v - jnp.mean(v, axis=-1, keepdims=True)
    vn = vc * lax.rsqrt(jnp.mean(vc * vc, axis=-1, keepdims=True) + EPS) * lg_ref[...] + lb_ref[...]
    vn = vn.astype(BF16)
    for h in range(A_HEADS):
        if h == A_HEADS // 2:
            fill()
        hs = slice(h * A_HEAD_DIM, (h + 1) * A_HEAD_DIM)
        s = _dot(ws_ref[h], vn[:, hs]) + bs_ref[:, hs]
        o_ref[rows, hs] = (_gelu(u_ref[rows, hs]) * s).astype(BF16)


def _mixer(p, act, dt, prev_f, prev_b, x2d, gate, w_out, consts, extra, *, nb, nc):
    steps_per_batch = nc // CHUNKS_PER_STEP
    n_blocks = nb * steps_per_batch
    cur = lambda t: jnp.minimum(t, n_blocks - 1)
    lag = lambda t: jnp.maximum(t - 1, 0)
    lane_blk = lambda k: pl.BlockSpec((STEP_ROWS, A_DIM), lambda t: (cur(t), k))
    cur_rows = lambda width: pl.BlockSpec((STEP_ROWS, width), lambda t: (cur(t), 0))
    lag_rows = pl.BlockSpec((STEP_ROWS, D_MODEL), lambda t: (lag(t), 0))
    lag_mod = pl.BlockSpec((None, 1, D_MODEL), lambda t: (lag(t) // steps_per_batch, 0, 0))
    state_spec = pl.BlockSpec((CHUNKS_PER_STEP, SSD_STATE, SSD_DIM), lambda t: (cur(t), 0, 0))
    whole = lambda a: pl.BlockSpec(a.shape, lambda t: (0,) * a.ndim)
    params = tuple(consts[2:]) + tuple(extra)
    args = (p, p, p, act, dt, prev_f, prev_b, x2d, gate, w_out) + params
    in_specs = [lane_blk(0), lane_blk(1), lane_blk(2), cur_rows(XBC_DIM), cur_rows(LANES),
                state_spec, state_spec, lag_rows, lag_mod, whole(w_out)]
    in_specs += [whole(a) for a in params]
    t_rows = n_blocks * STEP_ROWS
    return pl.pallas_call(
        _mixer_kernel,
        grid=(n_blocks + 1,),
        in_specs=in_specs,
        out_specs=lag_rows,
        out_shape=jax.ShapeDtypeStruct((t_rows, D_MODEL), F32),
        scratch_shapes=[pltpu.VMEM((STEP_ROWS, A_DIM + SSD_DIM), BF16),
                        pltpu.VMEM((STEP_ROWS, A_DIM + SSD_DIM), BF16)],
        compiler_params=_params(("arbitrary",)),
        name="mixer",
    )(*args)


def _ffn_weights(w_gate, w_up, w_down):
    return w_gate.astype(BF16), w_up.astype(BF16), w_down.astype(BF16)


def _ssd_consts(conv_w, conv_b, dt_bias, a_log):
    lanes_pad = LANES - 2 * SSD_HEADS
    bias_row = jnp.pad(dt_bias.reshape(1, 2 * SSD_HEADS), ((0, 0), (0, lanes_pad)))
    a_row = jnp.pad(-jnp.exp(a_log.reshape(1, 2 * SSD_HEADS)), ((0, 0), (0, lanes_pad)))
    tl = jnp.asarray(np.tril(np.ones((CHUNK, CHUNK), np.float32)))
    head_of_lane = np.arange(SSD_DIM) // SSD_HEAD_DIM
    sel = np.arange(LANES)[:, None] == head_of_lane[None, :]
    sel_b = np.arange(LANES)[:, None] == (head_of_lane[None, :] + SSD_HEADS)
    ef = jnp.asarray(sel.astype(np.float32)).astype(BF16)
    eb = jnp.asarray(sel_b.astype(np.float32)).astype(BF16)
    return (conv_w, conv_b.reshape(1, XBC_DIM), bias_row, a_row, tl, tl.T, ef, eb)


def kernel(x, c, ctx, c_ctx, w_mod, b_mod, norm_ffn1, ffn1_w_gate, ffn1_w_up, ffn1_w_down,
           norm_mix, w_in, conv_w, conv_b, dt_bias, a_log, d_skip, ssd_norm_g, gmlp_norm_g,
           gmlp_norm_b, gmlp_w_s, gmlp_b_s, w_out, norm_ffn2, ffn2_w_gate, ffn2_w_up,
           ffn2_w_down, norm_final):
    assert w_mod.shape[0] == 1, "single-layer block"
    bsz, seq, _ = x.shape
    ctx_len = ctx.shape[1]
    nc_x = seq // CHUNK
    nc_c = ctx_len // CHUNK

    pad_rows = (-(bsz + 1)) % SUBLANES
    c_all = jnp.concatenate([c, c_ctx[None, :], jnp.zeros((pad_rows, D_MODEL), F32)], axis=0)
    mod = _modulation(c_all, w_mod[0], b_mod[0]).reshape(-1, N_MOD, D_MODEL)
    mod_x = [mod[:bsz, k][:, None, :] for k in range(N_MOD)]
    mod_c = [mod[bsz:bsz + 1, k][:, None, :] for k in range(N_MOD)]

    row = lambda a: a.reshape(1, -1)
    x2d = x.reshape(bsz * seq, D_MODEL)
    c2d = ctx.reshape(bsz * ctx_len, D_MODEL)

    w1 = _ffn_weights(ffn1_w_gate[0], ffn1_w_up[0], ffn1_w_down[0])
    g1 = row(norm_ffn1[0])
    x1 = _ffn(x2d, mod_x[0], mod_x[1], mod_x[2], g1, *w1, g1, rows_per_mod=seq,
              final_norm=False)
    c1 = _ffn(c2d, mod_c[0], mod_c[1], mod_c[2], g1, *w1, g1, rows_per_mod=bsz * ctx_len,
              final_norm=False)

    w_proj = w_in[0].astype(BF16)
    w_dt = jnp.pad(w_proj[:, MAIN_COLS:], ((0, 0), (0, LANES - 2 * SSD_HEADS)))
    gm = row(norm_mix[0])
    main_blocks = MAIN_COLS // XBC_DIM
    px, dtx = _inproj(x1, mod_x[3], mod_x[4], gm, w_proj, w_dt, rows_per_mod=seq, first_block=0,
                      n_blocks=main_blocks)
    pc, dtc = _inproj(c1, mod_c[3], mod_c[4], gm, w_proj, w_dt, rows_per_mod=bsz * ctx_len,
                      first_block=main_blocks - 1, n_blocks=1)

    consts = _ssd_consts(conv_w[0], conv_b[0], dt_bias[0], a_log[0])
    h0 = jnp.zeros((bsz, SSD_STATE, SSD_DIM), F32)
    _, _, _, hf, hb = _scan(pc, dtc, h0, h0, consts, nb=bsz, nc=nc_c, col=0)
    act, prev_f, prev_b, _, _ = _scan(px, dtx, hf, hb, consts, nb=bsz, nc=nc_x,
                                      col=(MAIN_COLS - XBC_DIM) // XBC_DIM)

    extra = (row(jnp.repeat(d_skip[0], SSD_HEAD_DIM)), row(ssd_norm_g[0]), row(gmlp_norm_g[0]),
             row(gmlp_norm_b[0]), gmlp_w_s[0].astype(BF16),
             jnp.repeat(gmlp_b_s[0].T, A_HEAD_DIM, axis=1))
    x2 = _mixer(px, act, dtx, prev_f, prev_b, x1, mod_x[5], w_out[0].astype(BF16), consts, extra,
                nb=bsz, nc=nc_x)

    w2 = _ffn_weights(ffn2_w_gate[0], ffn2_w_up[0], ffn2_w_down[0])
    out = _ffn(x2, mod_x[6], mod_x[7], mod_x[8], row(norm_ffn2[0]), *w2, row(norm_final),
               rows_per_mod=seq, final_norm=True)
    return out.reshape(bsz, seq, D_MODEL)
```

```python
import functools
import math

import jax
import jax.numpy as jnp
import numpy as np
from jax import lax
from jax.experimental import pallas as pl
from jax.experimental.pallas import tpu as pltpu

F32 = jnp.float32
BF16 = jnp.bfloat16

D_MODEL = 2048
N_MOD = 9
D_FF = 5504
A_HEADS = 8
A_HEAD_DIM = 128
A_DIM = A_HEADS * A_HEAD_DIM
SSD_HEADS = 16
SSD_HEAD_DIM = 64
SSD_DIM = SSD_HEADS * SSD_HEAD_DIM
SSD_GROUPS = 2
SSD_STATE = 128
GROUP_DIM = SSD_DIM // SSD_GROUPS
CHUNK = 128
CONV_W = 5
XBC_DIM = SSD_DIM + 2 * SSD_GROUPS * SSD_STATE
MAIN_COLS = 2 * A_DIM + SSD_DIM + XBC_DIM
EPS = 1e-6

LANES = 128
SUBLANES = 8
VMEM_LIMIT_BYTES = 60 * 1024 * 1024

HALO = 16
PAD_ROWS = CHUNK + 2 * HALO


def _params(sem):
    return pltpu.CompilerParams(dimension_semantics=sem, vmem_limit_bytes=VMEM_LIMIT_BYTES)


def _dot(a, b):
    return jnp.dot(a, b, preferred_element_type=F32)


def _silu(x):
    return x * jax.nn.sigmoid(x)


def _gelu(x):
    return 0.5 * x * (1.0 + lax.erf(x * (1.0 / math.sqrt(2.0))))


def _softplus(x):
    return jnp.maximum(x, 0.0) + jnp.log1p(jnp.exp(-jnp.abs(x)))


def _rms(x, g):
    return x * lax.rsqrt(jnp.mean(x * x, axis=-1, keepdims=True) + EPS) * g


def _mod_kernel(c_ref, w_ref, b_ref, o_ref):
    o_ref[...] = _dot(_silu(c_ref[...]), w_ref[...]) + b_ref[...]


def _modulation(c_all, w_mod, b_mod, tn=1024):
    rows = c_all.shape[0]
    n = w_mod.shape[1]
    return pl.pallas_call(
        _mod_kernel,
        grid=(n // tn,),
        in_specs=[pl.BlockSpec((rows, D_MODEL), lambda j: (0, 0)),
                  pl.BlockSpec((D_MODEL, tn), lambda j: (0, j)),
                  pl.BlockSpec((1, tn), lambda j: (0, j))],
        out_specs=pl.BlockSpec((rows, tn), lambda j: (0, j)),
        out_shape=jax.ShapeDtypeStruct((rows, n), F32),
        compiler_params=_params(("arbitrary",)),
        name="mod",
    )(c_all, w_mod, b_mod.reshape(1, n))


FFN_TM = 1024
FFN_TF = 512


def _norm_mod(x, g, shift, scale):
    return (_rms(x, g) * (1.0 + scale) + shift).astype(BF16)


def _ffn_kernel(x_ref, shift_ref, scale_ref, gate_ref, g_ref, wg_ref, wu_ref, wd_ref, gf_ref,
                o_ref, h_ref, *, final_norm, last_cols):
    j = pl.program_id(1)
    last = pl.num_programs(1) - 1

    @pl.when(j == 0)
    def _():
        h_ref[...] = _norm_mod(x_ref[...], g_ref[...], shift_ref[...], scale_ref[...])

    def partial_ffn(cols):
        h = h_ref[...]
        gate = _dot(h, wg_ref[:, 0:cols])
        up = _dot(h, wu_ref[:, 0:cols])
        a = (_silu(gate) * up).astype(BF16)
        return _dot(a, wd_ref[0:cols, :])

    @pl.when(j == 0)
    def _():
        o_ref[...] = partial_ffn(FFN_TF)

    @pl.when((j > 0) & (j < last))
    def _():
        o_ref[...] += partial_ffn(FFN_TF)

    @pl.when(j == last)
    def _():
        f = o_ref[...] + partial_ffn(last_cols)
        r = x_ref[...] + (0.5 * gate_ref[...]) * f
        if final_norm:
            r = _rms(r, gf_ref[...])
        o_ref[...] = r


def _ffn(x2d, shift, scale, gate, g, wg, wu, wd, gf, *, rows_per_mod, final_norm):
    t = x2d.shape[0]
    tm, tf = min(FFN_TM, rows_per_mod), FFN_TF
    nf = pl.cdiv(D_FF, tf)
    assert nf >= 3 and t % tm == 0 and rows_per_mod % tm == 0
    tiles_per_mod = rows_per_mod // tm
    mod_spec = pl.BlockSpec((None, 1, D_MODEL), lambda i, j: (i // tiles_per_mod, 0, 0))
    row_spec = pl.BlockSpec((1, D_MODEL), lambda i, j: (0, 0))
    tile_spec = pl.BlockSpec((tm, D_MODEL), lambda i, j: (i, 0))
    w_specs = [pl.BlockSpec((D_MODEL, tf), lambda i, j: (0, j)),
               pl.BlockSpec((D_MODEL, tf), lambda i, j: (0, j)),
               pl.BlockSpec((tf, D_MODEL), lambda i, j: (j, 0))]
    args = (x2d, shift, scale, gate, g, wg, wu, wd, gf)
    in_specs = [tile_spec, mod_spec, mod_spec, mod_spec, row_spec] + w_specs + [row_spec]
    return pl.pallas_call(
        functools.partial(_ffn_kernel, final_norm=final_norm, last_cols=D_FF - (nf - 1) * tf),
        grid=(t // tm, nf),
        in_specs=in_specs,
        out_specs=tile_spec,
        out_shape=jax.ShapeDtypeStruct((t, D_MODEL), F32),
        scratch_shapes=[pltpu.VMEM((tm, D_MODEL), BF16)],
        compiler_params=_params(("arbitrary", "arbitrary")),
        name="ffn_final" if final_norm else "ffn",
    )(*args)


def _inproj_kernel(x_ref, shift_ref, scale_ref, g_ref, w_ref, wdt_ref, p_ref, dt_ref, h_ref):
    @pl.when(pl.program_id(1) == 0)
    def _():
        h_ref[...] = _norm_mod(x_ref[...], g_ref[...], shift_ref[...], scale_ref[...])
        p_ref[...] = _dot(h_ref[...], w_ref[...]).astype(BF16)
        dt_ref[...] = _dot(h_ref[...], wdt_ref[...])

    @pl.when(pl.program_id(1) > 0)
    def _():
        p_ref[...] = _dot(h_ref[...], w_ref[...]).astype(BF16)


def _inproj(x2d, shift, scale, g, w, w_dt, *, rows_per_mod, first_block, n_blocks, tn=XBC_DIM):
    t = x2d.shape[0]
    tm = min(1024, rows_per_mod)
    tiles_per_mod = rows_per_mod // tm
    mod_spec = pl.BlockSpec((None, 1, D_MODEL), lambda i, j: (i // tiles_per_mod, 0, 0))
    return pl.pallas_call(
        _inproj_kernel,
        grid=(t // tm, n_blocks),
        in_specs=[pl.BlockSpec((tm, D_MODEL), lambda i, j: (i, 0)), mod_spec, mod_spec,
                  pl.BlockSpec((1, D_MODEL), lambda i, j: (0, 0)),
                  pl.BlockSpec((D_MODEL, tn), lambda i, j: (0, first_block + j)),
                  pl.BlockSpec((D_MODEL, LANES), lambda i, j: (0, 0))],
        out_specs=[pl.BlockSpec((tm, tn), lambda i, j: (i, j)),
                   pl.BlockSpec((tm, LANES), lambda i, j: (i, 0))],
        out_shape=[jax.ShapeDtypeStruct((t, n_blocks * tn), BF16),
                   jax.ShapeDtypeStruct((t, LANES), F32)],
        scratch_shapes=[pltpu.VMEM((tm, D_MODEL), BF16)],
        compiler_params=_params(("arbitrary", "arbitrary")),
        name="inproj",
    )(x2d, shift, scale, g, w, w_dt)


def _conv_silu(cur_ref, r0, prev_rows, next_rows, w_ref, b_ref, pad_ref, sub):
    outs = []
    for s in range(XBC_DIM // LANES):
        ls = slice(s * LANES, (s + 1) * LANES)
        pad_ref[sub, s, 0:HALO, :] = prev_rows(ls)
        pad_ref[sub, s, HALO:HALO + CHUNK, :] = cur_ref[r0:r0 + CHUNK, ls].astype(F32)
        pad_ref[sub, s, HALO + CHUNK:PAD_ROWS, :] = next_rows(ls)
        acc = jnp.broadcast_to(b_ref[:, ls], (CHUNK, LANES))
        for k in range(CONV_W):
            start = HALO - CONV_W // 2 + k
            acc = acc + pad_ref[sub, s, pl.ds(start, CHUNK, stride=1), :] * w_ref[k:k + 1, ls]
        outs.append(_silu(acc))
    return jnp.concatenate(outs, axis=1)


def _dt_terms(dt_raw, bias_row, a_row, tl, tu):
    dt = _softplus(dt_raw + bias_row)
    a = dt * a_row
    s = jnp.dot(tl, a, precision=lax.Precision.HIGHEST, preferred_element_type=F32)
    r = jnp.dot(tu, a, precision=lax.Precision.HIGHEST, preferred_element_type=F32)
    is_fwd = lax.broadcasted_iota(jnp.int32, (CHUNK, LANES), 1) < SSD_HEADS
    cum = jnp.where(is_fwd, s, r)
    total = jnp.where(is_fwd[0:1], s[CHUNK - 1:CHUNK], r[0:1])
    return dt, cum, total


def _expand(v, e):
    hi = v.astype(BF16)
    lo = (v - hi.astype(F32)).astype(BF16)
    return _dot(hi, e) + _dot(lo, e)


PROJ_COLS = 256
CHUNKS_PER_STEP = 2
STEP_ROWS = CHUNKS_PER_STEP * CHUNK


def _scan_kernel(xc_ref, xp_ref, xn_ref, dt_ref, h0f_ref, h0b_ref, cw_ref, cb_ref, bias_ref,
                 arow_ref, tl_ref, tu_ref, ef_ref, eb_ref, act_ref, pf_ref, pb_ref, ff_ref, fb_ref,
                 pad_ref, sf_ref, sball_ref, dball_ref, *, nc):
    c = pl.program_id(1)
    nsteps = nc // CHUNKS_PER_STEP

    @pl.when(c == 0)
    def _():
        sf_ref[...] = h0f_ref[...]

    state = sf_ref[...]
    for sub in range(CHUNKS_PER_STEP):
        r0 = sub * CHUNK
        if sub == 0:
            prev_rows = lambda ls: jnp.where(c == 0, 0.0, xp_ref[:, ls].astype(F32))
        else:
            prev_rows = lambda ls, r0=r0: xc_ref[r0 - HALO:r0, ls].astype(F32)
        if sub == CHUNKS_PER_STEP - 1:
            next_rows = lambda ls: jnp.where(c == nsteps - 1, 0.0, xn_ref[:, ls].astype(F32))
        else:
            next_rows = lambda ls, r0=r0: xc_ref[r0 + CHUNK:r0 + CHUNK + HALO, ls].astype(F32)
        xb = _conv_silu(xc_ref, r0, prev_rows, next_rows, cw_ref, cb_ref, pad_ref, sub)
        act_ref[r0:r0 + CHUNK, :] = xb.astype(BF16)
        dt, cum, total = _dt_terms(dt_ref[r0:r0 + CHUNK, :], bias_ref[...], arow_ref[...],
                                   tl_ref[...], tu_ref[...])
        w = dt * jnp.exp(total - cum)
        decay = jnp.broadcast_to(jnp.exp(total), (SUBLANES, LANES))
        v = jnp.concatenate([w, decay], axis=0)
        xs = xb[:, 0:SSD_DIM]
        bt = []
        for g in range(SSD_GROUPS):
            lo = SSD_DIM + g * SSD_STATE
            bt.append(xb[:, lo:lo + SSD_STATE].T.astype(BF16))

        def chunk_state(e_ref):
            ex = _expand(v, e_ref[...])
            xw = (xs * ex[0:CHUNK]).astype(BF16)
            new = [_dot(bt[g], xw[:, g * GROUP_DIM:(g + 1) * GROUP_DIM])
                   for g in range(SSD_GROUPS)]
            return jnp.concatenate(new, axis=1), ex[CHUNK:CHUNK + SUBLANES]

        new_f, decay_f = chunk_state(ef_ref)
        pf_ref[sub] = state.astype(BF16)
        state = state * decay_f[0:1] + new_f

        new_b, decay_b = chunk_state(eb_ref)
        sball_ref[c * CHUNKS_PER_STEP + sub] = new_b
        dball_ref[c * CHUNKS_PER_STEP + sub] = decay_b
    sf_ref[...] = state

    @pl.when(c == nsteps - 1)
    def _():
        ff_ref[...] = sf_ref[...]
        sb = h0b_ref[...]
        for cc in reversed(range(nc)):
            pb_ref[cc] = sb.astype(BF16)
            sb = sb * dball_ref[cc, 0:1, :] + sball_ref[cc]
        fb_ref[...] = sb


def _halo_specs(nsteps, col):
    per = STEP_ROWS // HALO
    cur = pl.BlockSpec((STEP_ROWS, XBC_DIM), lambda b, c: (b * nsteps + c, col))
    prev = pl.BlockSpec((HALO, XBC_DIM),
                        lambda b, c: (jnp.maximum((b * nsteps + c) * per - 1, 0), col))
    nxt = pl.BlockSpec((HALO, XBC_DIM),
                       lambda b, c: (jnp.minimum((b * nsteps + c + 1) * per,
                                                 (b * nsteps + nsteps) * per - 1), col))
    return cur, prev, nxt


def _full_spec(a):
    return pl.BlockSpec(a.shape, lambda b, c: (0,) * a.ndim)


def _scan(p, dt, h0f, h0b, consts, *, nb, nc, col):
    assert nc % CHUNKS_PER_STEP == 0
    nsteps = nc // CHUNKS_PER_STEP
    state_spec = pl.BlockSpec((None, SSD_STATE, SSD_DIM), lambda b, c: (b, 0, 0))
    step_rows = lambda width: pl.BlockSpec((STEP_ROWS, width), lambda b, c: (b * nsteps + c, 0))
    args = (p, p, p, dt, h0f, h0b) + tuple(consts)
    in_specs = list(_halo_specs(nsteps, col)) + [step_rows(LANES), state_spec, state_spec]
    in_specs += [_full_spec(a) for a in consts]
    return pl.pallas_call(
        functools.partial(_scan_kernel, nc=nc),
        grid=(nb, nsteps),
        in_specs=in_specs,
        out_specs=[step_rows(XBC_DIM),
                   pl.BlockSpec((CHUNKS_PER_STEP, SSD_STATE, SSD_DIM),
                                lambda b, c: (b * nsteps + c, 0, 0)),
                   pl.BlockSpec((nc, SSD_STATE, SSD_DIM), lambda b, c: (b, 0, 0)),
                   state_spec, state_spec],
        out_shape=[jax.ShapeDtypeStruct((nb * nc * CHUNK, XBC_DIM), BF16),
                   jax.ShapeDtypeStruct((nb * nc, SSD_STATE, SSD_DIM), BF16),
                   jax.ShapeDtypeStruct((nb * nc, SSD_STATE, SSD_DIM), BF16),
                   jax.ShapeDtypeStruct((nb, SSD_STATE, SSD_DIM), F32),
                   jax.ShapeDtypeStruct((nb, SSD_STATE, SSD_DIM), F32)],
        scratch_shapes=[pltpu.VMEM((CHUNKS_PER_STEP, XBC_DIM // LANES, PAD_ROWS, LANES), F32),
                        pltpu.VMEM((SSD_STATE, SSD_DIM), F32),
                        pltpu.VMEM((nc, SSD_STATE, SSD_DIM), F32),
                        pltpu.VMEM((nc, SUBLANES, SSD_DIM), F32)],
        compiler_params=_params(("arbitrary", "arbitrary")),
        name="ssd_scan",
    )(*args)


def _mixer_kernel(u_ref, v_ref, z_ref, act_ref, dt_ref, pf_ref, pb_ref,
                  x_ref, gate_ref, wout_ref,
                  bias_ref, arow_ref, tl_ref, tu_ref, ef_ref, eb_ref,
                  dskip_ref, ng_ref, lg_ref, lb_ref, ws_ref, bs_ref,
                  o_ref, mix_ref, stage_ref):
    @pl.when(pl.program_id(0) == 0)
    def _():
        mix_ref[...] = jnp.zeros_like(mix_ref)

    def proj_piece(n):
        def run():
            cols = slice(n * PROJ_COLS, (n + 1) * PROJ_COLS)
            f = _dot(mix_ref[...], wout_ref[:, cols])
            o_ref[:, cols] = x_ref[:, cols] + gate_ref[:, cols] * f
        return run

    pieces = [proj_piece(n) for n in range(D_MODEL // PROJ_COLS)]
    per_chunk = len(pieces) // CHUNKS_PER_STEP
    for sub in range(CHUNKS_PER_STEP):
        mine = pieces[sub * per_chunk:(sub + 1) * per_chunk]
        _mixer_chunk(sub, u_ref, v_ref, z_ref, act_ref, dt_ref, pf_ref, pb_ref, bias_ref, arow_ref,
                     tl_ref, tu_ref, ef_ref, eb_ref, dskip_ref, ng_ref, lg_ref, lb_ref, ws_ref,
                     bs_ref, stage_ref, mine)
        while mine:
            mine.pop(0)()
    mix_ref[...] = stage_ref[...]


def _mixer_chunk(sub, u_ref, v_ref, z_ref, act_ref, dt_ref, pf_ref, pb_ref, bias_ref, arow_ref,
                 tl_ref, tu_ref, ef_ref, eb_ref, dskip_ref, ng_ref, lg_ref, lb_ref, ws_ref, bs_ref,
                 o_ref, fillers):
    def fill():
        if fillers:
            fillers.pop(0)()

    rows = slice(sub * CHUNK, (sub + 1) * CHUNK)
    xs = act_ref[rows, 0:SSD_DIM].astype(F32)
    gn = SSD_GROUPS * SSD_STATE
    bm = act_ref[rows, SSD_DIM:SSD_DIM + gn]
    cm = act_ref[rows, SSD_DIM + gn:XBC_DIM]

    dt, cum, _ = _dt_terms(dt_ref[rows, :], bias_ref[...], arow_ref[...], tl_ref[...], tu_ref[...])
    ecum = jnp.exp(cum)
    ecum_f = _expand(ecum, ef_ref[...])
    ecum_b = _expand(ecum, eb_ref[...])
    dt_t = dt.T
    cum_t = cum.T

    lane = lax.broadcasted_iota(jnp.int32, (CHUNK, SSD_DIM), 1)
    first_head = (lane % LANES) < SSD_HEAD_DIM
    xs_even = jnp.where(first_head, xs, 0.0).astype(BF16)
    xs_odd = jnp.where(first_head, 0.0, xs).astype(BF16)

    row = lax.broadcasted_iota(jnp.int32, (CHUNK, CHUNK), 0)
    col = lax.broadcasted_iota(jnp.int32, (CHUNK, CHUNK), 1)
    causal = row >= col
    anticausal = row <= col

    def decay_matrix(h):
        lf = jnp.exp(jnp.minimum(cum[:, h:h + 1] - cum_t[h:h + 1, :], 0.0))
        lf = jnp.where(causal, lf, 0.0) * dt_t[h:h + 1, :]
        hb = SSD_HEADS + h
        lb = jnp.exp(jnp.minimum(cum[:, hb:hb + 1] - cum_t[hb:hb + 1, :], 0.0))
        lb = jnp.where(anticausal, lb, 0.0) * dt_t[hb:hb + 1, :]
        return lf + lb

    pf = pf_ref[sub]
    pb = pb_ref[sub]
    heads_per_group = SSD_HEADS // SSD_GROUPS
    y_parts = []
    for g in range(SSD_GROUPS):
        cg = cm[:, g * SSD_STATE:(g + 1) * SSD_STATE]
        bg = bm[:, g * SSD_STATE:(g + 1) * SSD_STATE]
        cb_g = lax.dot_general(cg, bg, (((1,), (1,)), ((), ())), preferred_element_type=F32)
        gs = slice(g * GROUP_DIM, (g + 1) * GROUP_DIM)
        y_off = ecum_f[:, gs] * _dot(cg, pf[:, gs]) + ecum_b[:, gs] * _dot(cg, pb[:, gs])
        diag = []
        for j in range(heads_per_group // 2):
            if j == heads_per_group // 4:
                fill()
            h0 = g * heads_per_group + 2 * j
            m_pair = jnp.concatenate([(cb_g * decay_matrix(h0)).astype(BF16),
                                      (cb_g * decay_matrix(h0 + 1)).astype(BF16)], axis=1)
            ls = slice(h0 * SSD_HEAD_DIM, (h0 + 2) * SSD_HEAD_DIM)
            rhs = jnp.concatenate([xs_even[:, ls], xs_odd[:, ls]], axis=0)
            diag.append(_dot(m_pair, rhs))
        y_parts.append(jnp.concatenate(diag, axis=1) + y_off)
    y = jnp.concatenate(y_parts, axis=1) + dskip_ref[...] * xs

    yz = y * _silu(z_ref[rows, :].astype(F32))
    normed = []
    for g in range(SSD_GROUPS):
        part = yz[:, g * GROUP_DIM:(g + 1) * GROUP_DIM]
        normed.append(part * lax.rsqrt(jnp.mean(part * part, axis=-1, keepdims=True) + EPS))
    ssd_out = jnp.concatenate(normed, axis=1) * ng_ref[...]
    o_ref[rows, A_DIM:A_DIM + SSD_DIM] = ssd_out.astype(BF16)

    fill()
    v = _gelu(v_ref[rows, :].astype(F32))
    vc = v - jnp.mean(v, axis=-1, keepdims=True)
    vn = vc * lax.rsqrt(jnp.mean(vc * vc, axis=-1, keepdims=True) + EPS) * lg_ref[...] + lb_ref[...]
    vn = vn.astype(BF16)
    for h in range(A_HEADS):
        if h == A_HEADS // 2:
            fill()
        hs = slice(h * A_HEAD_DIM, (h + 1) * A_HEAD_DIM)
        s = _dot(ws_ref[h], vn[:, hs]) + bs_ref[:, hs]
        o_ref[rows, hs] = (_gelu(u_ref[rows, hs].astype(F32)) * s).astype(BF16)


def _mixer(p, act, dt, prev_f, prev_b, x2d, gate, w_out, consts, extra, *, nb, nc):
    steps_per_batch = nc // CHUNKS_PER_STEP
    n_blocks = nb * steps_per_batch
    cur = lambda t: jnp.minimum(t, n_blocks - 1)
    lag = lambda t: jnp.maximum(t - 1, 0)
    lane_blk = lambda k: pl.BlockSpec((STEP_ROWS, A_DIM), lambda t: (cur(t), k))
    cur_rows = lambda width: pl.BlockSpec((STEP_ROWS, width), lambda t: (cur(t), 0))
    lag_rows = pl.BlockSpec((STEP_ROWS, D_MODEL), lambda t: (lag(t), 0))
    lag_mod = pl.BlockSpec((None, 1, D_MODEL), lambda t: (lag(t) // steps_per_batch, 0, 0))
    state_spec = pl.BlockSpec((CHUNKS_PER_STEP, SSD_STATE, SSD_DIM), lambda t: (cur(t), 0, 0))
    whole = lambda a: pl.BlockSpec(a.shape, lambda t: (0,) * a.ndim)
    params = tuple(consts[2:]) + tuple(extra)
    args = (p, p, p, act, dt, prev_f, prev_b, x2d, gate, w_out) + params
    in_specs = [lane_blk(0), lane_blk(1), lane_blk(2), cur_rows(XBC_DIM), cur_rows(LANES),
                state_spec, state_spec, lag_rows, lag_mod, whole(w_out)]
    in_specs += [whole(a) for a in params]
    t_rows = n_blocks * STEP_ROWS
    return pl.pallas_call(
        _mixer_kernel,
        grid=(n_blocks + 1,),
        in_specs=in_specs,
        out_specs=lag_rows,
        out_shape=jax.ShapeDtypeStruct((t_rows, D_MODEL), F32),
        scratch_shapes=[pltpu.VMEM((STEP_ROWS, A_DIM + SSD_DIM), BF16),
                        pltpu.VMEM((STEP_ROWS, A_DIM + SSD_DIM), BF16)],
        compiler_params=_params(("arbitrary",)),
        name="mixer",
    )(*args)


def _ffn_weights(w_gate, w_up, w_down):
    return w_gate.astype(BF16), w_up.astype(BF16), w_down.astype(BF16)


def _ssd_consts(conv_w, conv_b, dt_bias, a_log):
    lanes_pad = LANES - 2 * SSD_HEADS
    bias_row = jnp.pad(dt_bias.reshape(1, 2 * SSD_HEADS), ((0, 0), (0, lanes_pad)))
    a_row = jnp.pad(-jnp.exp(a_log.reshape(1, 2 * SSD_HEADS)), ((0, 0), (0, lanes_pad)))
    tl = jnp.asarray(np.tril(np.ones((CHUNK, CHUNK), np.float32)))
    head_of_lane = np.arange(SSD_DIM) // SSD_HEAD_DIM
    sel = np.arange(LANES)[:, None] == head_of_lane[None, :]
    sel_b = np.arange(LANES)[:, None] == (head_of_lane[None, :] + SSD_HEADS)
    ef = jnp.asarray(sel.astype(np.float32)).astype(BF16)
    eb = jnp.asarray(sel_b.astype(np.float32)).astype(BF16)
    return (conv_w, conv_b.reshape(1, XBC_DIM), bias_row, a_row, tl, tl.T, ef, eb)


def kernel(x, c, ctx, c_ctx, w_mod, b_mod, norm_ffn1, ffn1_w_gate, ffn1_w_up, ffn1_w_down,
           norm_mix, w_in, conv_w, conv_b, dt_bias, a_log, d_skip, ssd_norm_g, gmlp_norm_g,
           gmlp_norm_b, gmlp_w_s, gmlp_b_s, w_out, norm_ffn2, ffn2_w_gate, ffn2_w_up,
           ffn2_w_down, norm_final):
    assert w_mod.shape[0] == 1, "single-layer block"
    bsz, seq, _ = x.shape
    ctx_len = ctx.shape[1]
    nc_x = seq // CHUNK
    nc_c = ctx_len // CHUNK

    pad_rows = (-(bsz + 1)) % SUBLANES
    c_all = jnp.concatenate([c, c_ctx[None, :], jnp.zeros((pad_rows, D_MODEL), F32)], axis=0)
    mod = _modulation(c_all, w_mod[0], b_mod[0]).reshape(-1, N_MOD, D_MODEL)
    mod_x = [mod[:bsz, k][:, None, :] for k in range(N_MOD)]
    mod_c = [mod[bsz:bsz + 1, k][:, None, :] for k in range(N_MOD)]

    row = lambda a: a.reshape(1, -1)
    x2d = x.reshape(bsz * seq, D_MODEL)
    c2d = ctx.reshape(bsz * ctx_len, D_MODEL)

    w1 = _ffn_weights(ffn1_w_gate[0], ffn1_w_up[0], ffn1_w_down[0])
    g1 = row(norm_ffn1[0])
    x1 = _ffn(x2d, mod_x[0], mod_x[1], mod_x[2], g1, *w1, g1, rows_per_mod=seq,
              final_norm=False)
    c1 = _ffn(c2d, mod_c[0], mod_c[1], mod_c[2], g1, *w1, g1, rows_per_mod=bsz * ctx_len,
              final_norm=False)

    w_proj = w_in[0].astype(BF16)
    w_dt = jnp.pad(w_proj[:, MAIN_COLS:], ((0, 0), (0, LANES - 2 * SSD_HEADS)))
    gm = row(norm_mix[0])
    main_blocks = MAIN_COLS // XBC_DIM
    px, dtx = _inproj(x1, mod_x[3], mod_x[4], gm, w_proj, w_dt, rows_per_mod=seq, first_block=0,
                      n_blocks=main_blocks)
    pc, dtc = _inproj(c1, mod_c[3], mod_c[4], gm, w_proj, w_dt, rows_per_mod=bsz * ctx_len,
                      first_block=main_blocks - 1, n_blocks=1)

    consts = _ssd_consts(conv_w[0], conv_b[0], dt_bias[0], a_log[0])
    h0 = jnp.zeros((bsz, SSD_STATE, SSD_DIM), F32)
    _, _, _, hf, hb = _scan(pc, dtc, h0, h0, consts, nb=bsz, nc=nc_c, col=0)
    act, prev_f, prev_b, _, _ = _scan(px, dtx, hf, hb, consts, nb=bsz, nc=nc_x,
                                      col=(MAIN_COLS - XBC_DIM) // XBC_DIM)

    extra = (row(jnp.repeat(d_skip[0], SSD_HEAD_DIM)), row(ssd_norm_g[0]), row(gmlp_norm_g[0]),
             row(gmlp_norm_b[0]), gmlp_w_s[0].astype(BF16),
             jnp.repeat(gmlp_b_s[0].T, A_HEAD_DIM, axis=1))
    x2 = _mixer(px, act, dtx, prev_f, prev_b, x1, mod_x[5], w_out[0].astype(BF16), consts, extra,
                nb=bsz, nc=nc_x)

    w2 = _ffn_weights(ffn2_w_gate[0], ffn2_w_up[0], ffn2_w_down[0])
    out = _ffn(x2, mod_x[6], mod_x[7], mod_x[8], row(norm_ffn2[0]), *w2, row(norm_final),
               rows_per_mod=seq, final_norm=True)
    return out.reshape(bsz, seq, D_MODEL)
```

```python
import functools
import math

import jax
import jax.numpy as jnp
import numpy as np
from jax import lax
from jax.experimental import pallas as pl
from jax.experimental.pallas import tpu as pltpu

F32 = jnp.float32
BF16 = jnp.bfloat16

D_MODEL = 2048
N_MOD = 9
D_FF = 5504
A_HEADS = 8
A_HEAD_DIM = 128
A_DIM = A_HEADS * A_HEAD_DIM
SSD_HEADS = 16
SSD_HEAD_DIM = 64
SSD_DIM = SSD_HEADS * SSD_HEAD_DIM
SSD_GROUPS = 2
SSD_STATE = 128
GROUP_DIM = SSD_DIM // SSD_GROUPS
CHUNK = 128
CONV_W = 5
XBC_DIM = SSD_DIM + 2 * SSD_GROUPS * SSD_STATE
MAIN_COLS = 2 * A_DIM + SSD_DIM + XBC_DIM
EPS = 1e-6

LANES = 128
SUBLANES = 8
VMEM_LIMIT_BYTES = 60 * 1024 * 1024

HALO = SUBLANES
PAD_ROWS = CHUNK + 2 * HALO


def _params(sem):
    return pltpu.CompilerParams(dimension_semantics=sem, vmem_limit_bytes=VMEM_LIMIT_BYTES)


def _dot(a, b):
    return jnp.dot(a, b, preferred_element_type=F32)


def _silu(x):
    return x * jax.nn.sigmoid(x)


def _gelu(x):
    return 0.5 * x * (1.0 + lax.erf(x * (1.0 / math.sqrt(2.0))))


def _softplus(x):
    return jnp.maximum(x, 0.0) + jnp.log1p(jnp.exp(-jnp.abs(x)))


def _rms(x, g):
    return x * lax.rsqrt(jnp.mean(x * x, axis=-1, keepdims=True) + EPS) * g


def _mod_kernel(c_ref, w_ref, b_ref, o_ref):
    o_ref[...] = _dot(_silu(c_ref[...]), w_ref[...]) + b_ref[...]


def _modulation(c_all, w_mod, b_mod, tn=1024):
    rows = c_all.shape[0]
    n = w_mod.shape[1]
    return pl.pallas_call(
        _mod_kernel,
        grid=(n // tn,),
        in_specs=[pl.BlockSpec((rows, D_MODEL), lambda j: (0, 0)),
                  pl.BlockSpec((D_MODEL, tn), lambda j: (0, j)),
                  pl.BlockSpec((1, tn), lambda j: (0, j))],
        out_specs=pl.BlockSpec((rows, tn), lambda j: (0, j)),
        out_shape=jax.ShapeDtypeStruct((rows, n), F32),
        compiler_params=_params(("arbitrary",)),
        name="mod",
    )(c_all, w_mod, b_mod.reshape(1, n))


FFN_TM = 1024
FFN_TF = 512


def _norm_mod(x, g, shift, scale):
    return (_rms(x, g) * (1.0 + scale) + shift).astype(BF16)


def _ffn_kernel(x_ref, shift_ref, scale_ref, gate_ref, g_ref, wg_ref, wu_ref, wd_ref, gf_ref,
                o_ref, h_ref, *, final_norm, last_cols):
    j = pl.program_id(1)
    last = pl.num_programs(1) - 1

    @pl.when(j == 0)
    def _():
        h_ref[...] = _norm_mod(x_ref[...], g_ref[...], shift_ref[...], scale_ref[...])

    def partial_ffn(cols):
        h = h_ref[...]
        gate = _dot(h, wg_ref[:, 0:cols])
        up = _dot(h, wu_ref[:, 0:cols])
        a = (_silu(gate) * up).astype(BF16)
        return _dot(a, wd_ref[0:cols, :])

    @pl.when(j == 0)
    def _():
        o_ref[...] = partial_ffn(FFN_TF)

    @pl.when((j > 0) & (j < last))
    def _():
        o_ref[...] += partial_ffn(FFN_TF)

    @pl.when(j == last)
    def _():
        f = o_ref[...] + partial_ffn(last_cols)
        r = x_ref[...] + (0.5 * gate_ref[...]) * f
        if final_norm:
            r = _rms(r, gf_ref[...])
        o_ref[...] = r


def _ffn(x2d, shift, scale, gate, g, wg, wu, wd, gf, *, rows_per_mod, final_norm):
    t = x2d.shape[0]
    tm, tf = min(FFN_TM, rows_per_mod), FFN_TF
    nf = pl.cdiv(D_FF, tf)
    assert nf >= 3 and t % tm == 0 and rows_per_mod % tm == 0
    tiles_per_mod = rows_per_mod // tm
    mod_spec = pl.BlockSpec((None, 1, D_MODEL), lambda i, j: (i // tiles_per_mod, 0, 0))
    row_spec = pl.BlockSpec((1, D_MODEL), lambda i, j: (0, 0))
    tile_spec = pl.BlockSpec((tm, D_MODEL), lambda i, j: (i, 0))
    w_specs = [pl.BlockSpec((D_MODEL, tf), lambda i, j: (0, j)),
               pl.BlockSpec((D_MODEL, tf), lambda i, j: (0, j)),
               pl.BlockSpec((tf, D_MODEL), lambda i, j: (j, 0))]
    args = (x2d, shift, scale, gate, g, wg, wu, wd, gf)
    in_specs = [tile_spec, mod_spec, mod_spec, mod_spec, row_spec] + w_specs + [row_spec]
    return pl.pallas_call(
        functools.partial(_ffn_kernel, final_norm=final_norm, last_cols=D_FF - (nf - 1) * tf),
        grid=(t // tm, nf),
        in_specs=in_specs,
        out_specs=tile_spec,
        out_shape=jax.ShapeDtypeStruct((t, D_MODEL), F32),
        scratch_shapes=[pltpu.VMEM((tm, D_MODEL), BF16)],
        compiler_params=_params(("arbitrary", "arbitrary")),
        name="ffn_final" if final_norm else "ffn",
    )(*args)


def _inproj_kernel(x_ref, shift_ref, scale_ref, g_ref, w_ref, wdt_ref, p_ref, dt_ref, h_ref):
    @pl.when(pl.program_id(1) == 0)
    def _():
        h_ref[...] = _norm_mod(x_ref[...], g_ref[...], shift_ref[...], scale_ref[...])
        p_ref[...] = _dot(h_ref[...], w_ref[...])
        dt_ref[...] = _dot(h_ref[...], wdt_ref[...])

    @pl.when(pl.program_id(1) > 0)
    def _():
        p_ref[...] = _dot(h_ref[...], w_ref[...])


def _inproj(x2d, shift, scale, g, w, w_dt, *, rows_per_mod, first_block, n_blocks, tn=XBC_DIM):
    t = x2d.shape[0]
    tm = min(1024, rows_per_mod)
    tiles_per_mod = rows_per_mod // tm
    mod_spec = pl.BlockSpec((None, 1, D_MODEL), lambda i, j: (i // tiles_per_mod, 0, 0))
    return pl.pallas_call(
        _inproj_kernel,
        grid=(t // tm, n_blocks),
        in_specs=[pl.BlockSpec((tm, D_MODEL), lambda i, j: (i, 0)), mod_spec, mod_spec,
                  pl.BlockSpec((1, D_MODEL), lambda i, j: (0, 0)),
                  pl.BlockSpec((D_MODEL, tn), lambda i, j: (0, first_block + j)),
                  pl.BlockSpec((D_MODEL, LANES), lambda i, j: (0, 0))],
        out_specs=[pl.BlockSpec((tm, tn), lambda i, j: (i, j)),
                   pl.BlockSpec((tm, LANES), lambda i, j: (i, 0))],
        out_shape=[jax.ShapeDtypeStruct((t, n_blocks * tn), F32),
                   jax.ShapeDtypeStruct((t, LANES), F32)],
        scratch_shapes=[pltpu.VMEM((tm, D_MODEL), BF16)],
        compiler_params=_params(("arbitrary", "arbitrary")),
        name="inproj",
    )(x2d, shift, scale, g, w, w_dt)


def _conv_silu(cur_ref, r0, prev_rows, next_rows, w_ref, b_ref, pad_ref, sub):
    outs = []
    for s in range(XBC_DIM // LANES):
        ls = slice(s * LANES, (s + 1) * LANES)
        pad_ref[sub, s, 0:HALO, :] = prev_rows(ls)
        pad_ref[sub, s, HALO:HALO + CHUNK, :] = cur_ref[r0:r0 + CHUNK, ls]
        pad_ref[sub, s, HALO + CHUNK:PAD_ROWS, :] = next_rows(ls)
        acc = jnp.broadcast_to(b_ref[:, ls], (CHUNK, LANES))
        for k in range(CONV_W):
            start = HALO - CONV_W // 2 + k
            acc = acc + pad_ref[sub, s, pl.ds(start, CHUNK, stride=1), :] * w_ref[k:k + 1, ls]
        outs.append(_silu(acc))
    return jnp.concatenate(outs, axis=1)


def _dt_terms(dt_raw, bias_row, a_row, tl, tu):
    dt = _softplus(dt_raw + bias_row)
    a = dt * a_row
    s = jnp.dot(tl, a, precision=lax.Precision.HIGHEST, preferred_element_type=F32)
    r = jnp.dot(tu, a, precision=lax.Precision.HIGHEST, preferred_element_type=F32)
    is_fwd = lax.broadcasted_iota(jnp.int32, (CHUNK, LANES), 1) < SSD_HEADS
    cum = jnp.where(is_fwd, s, r)
    total = jnp.where(is_fwd[0:1], s[CHUNK - 1:CHUNK], r[0:1])
    return dt, cum, total


def _expand(v, e):
    hi = v.astype(BF16)
    lo = (v - hi.astype(F32)).astype(BF16)
    return _dot(hi, e) + _dot(lo, e)


PROJ_COLS = 256
CHUNKS_PER_STEP = 2
STEP_ROWS = CHUNKS_PER_STEP * CHUNK


def _scan_kernel(xc_ref, xp_ref, xn_ref, dt_ref, h0f_ref, h0b_ref, cw_ref, cb_ref, bias_ref,
                 arow_ref, tl_ref, tu_ref, ef_ref, eb_ref, act_ref, dto_ref, cum_ref, pf_ref, pb_ref,
                 ff_ref, fb_ref, pad_ref, sf_ref, sball_ref, dball_ref, *, nc):
    c = pl.program_id(1)
    nsteps = nc // CHUNKS_PER_STEP

    @pl.when(c == 0)
    def _():
        sf_ref[...] = h0f_ref[...]

    state = sf_ref[...]
    for sub in range(CHUNKS_PER_STEP):
        r0 = sub * CHUNK
        if sub == 0:
            prev_rows = lambda ls: jnp.where(c == 0, 0.0, xp_ref[:, ls])
        else:
            prev_rows = lambda ls, r0=r0: xc_ref[r0 - HALO:r0, ls]
        if sub == CHUNKS_PER_STEP - 1:
            next_rows = lambda ls: jnp.where(c == nsteps - 1, 0.0, xn_ref[:, ls])
        else:
            next_rows = lambda ls, r0=r0: xc_ref[r0 + CHUNK:r0 + CHUNK + HALO, ls]
        xb = _conv_silu(xc_ref, r0, prev_rows, next_rows, cw_ref, cb_ref, pad_ref, sub)
        act_ref[r0:r0 + CHUNK, :] = xb
        dt, cum, total = _dt_terms(dt_ref[r0:r0 + CHUNK, :], bias_ref[...], arow_ref[...],
                                   tl_ref[...], tu_ref[...])
        dto_ref[r0:r0 + CHUNK, :] = dt
        cum_ref[r0:r0 + CHUNK, :] = cum
        w = dt * jnp.exp(total - cum)
        decay = jnp.broadcast_to(jnp.exp(total), (SUBLANES, LANES))
        v = jnp.concatenate([w, decay], axis=0)
        xs = xb[:, 0:SSD_DIM]
        bt = []
        for g in range(SSD_GROUPS):
            lo = SSD_DIM + g * SSD_STATE
            bt.append(xb[:, lo:lo + SSD_STATE].T.astype(BF16))

        def chunk_state(e_ref):
            ex = _expand(v, e_ref[...])
            xw = (xs * ex[0:CHUNK]).astype(BF16)
            new = [_dot(bt[g], xw[:, g * GROUP_DIM:(g + 1) * GROUP_DIM])
                   for g in range(SSD_GROUPS)]
            return jnp.concatenate(new, axis=1), ex[CHUNK:CHUNK + SUBLANES]

        new_f, decay_f = chunk_state(ef_ref)
        pf_ref[sub] = state.astype(BF16)
        state = state * decay_f[0:1] + new_f

        new_b, decay_b = chunk_state(eb_ref)
        sball_ref[c * CHUNKS_PER_STEP + sub] = new_b
        dball_ref[c * CHUNKS_PER_STEP + sub] = decay_b
    sf_ref[...] = state

    @pl.when(c == nsteps - 1)
    def _():
        ff_ref[...] = sf_ref[...]
        sb = h0b_ref[...]
        for cc in reversed(range(nc)):
            pb_ref[cc] = sb.astype(BF16)
            sb = sb * dball_ref[cc, 0:1, :] + sball_ref[cc]
        fb_ref[...] = sb


def _halo_specs(nsteps, col):
    per = STEP_ROWS // HALO
    cur = pl.BlockSpec((STEP_ROWS, XBC_DIM), lambda b, c: (b * nsteps + c, col))
    prev = pl.BlockSpec((HALO, XBC_DIM),
                        lambda b, c: (jnp.maximum((b * nsteps + c) * per - 1, 0), col))
    nxt = pl.BlockSpec((HALO, XBC_DIM),
                       lambda b, c: (jnp.minimum((b * nsteps + c + 1) * per,
                                                 (b * nsteps + nsteps) * per - 1), col))
    return cur, prev, nxt


def _full_spec(a):
    return pl.BlockSpec(a.shape, lambda b, c: (0,) * a.ndim)


def _scan(p, dt, h0f, h0b, consts, *, nb, nc, col):
    assert nc % CHUNKS_PER_STEP == 0
    nsteps = nc // CHUNKS_PER_STEP
    state_spec = pl.BlockSpec((None, SSD_STATE, SSD_DIM), lambda b, c: (b, 0, 0))
    step_rows = lambda width: pl.BlockSpec((STEP_ROWS, width), lambda b, c: (b * nsteps + c, 0))
    args = (p, p, p, dt, h0f, h0b) + tuple(consts)
    in_specs = list(_halo_specs(nsteps, col)) + [step_rows(LANES), state_spec, state_spec]
    in_specs += [_full_spec(a) for a in consts]
    return pl.pallas_call(
        functools.partial(_scan_kernel, nc=nc),
        grid=(nb, nsteps),
        in_specs=in_specs,
        out_specs=[step_rows(XBC_DIM), step_rows(LANES), step_rows(LANES),
                   pl.BlockSpec((CHUNKS_PER_STEP, SSD_STATE, SSD_DIM),
                                lambda b, c: (b * nsteps + c, 0, 0)),
                   pl.BlockSpec((nc, SSD_STATE, SSD_DIM), lambda b, c: (b, 0, 0)),
                   state_spec, state_spec],
        out_shape=[jax.ShapeDtypeStruct((nb * nc * CHUNK, XBC_DIM), F32),
                   jax.ShapeDtypeStruct((nb * nc * CHUNK, LANES), F32),
                   jax.ShapeDtypeStruct((nb * nc * CHUNK, LANES), F32),
                   jax.ShapeDtypeStruct((nb * nc, SSD_STATE, SSD_DIM), BF16),
                   jax.ShapeDtypeStruct((nb * nc, SSD_STATE, SSD_DIM), BF16),
                   jax.ShapeDtypeStruct((nb, SSD_STATE, SSD_DIM), F32),
                   jax.ShapeDtypeStruct((nb, SSD_STATE, SSD_DIM), F32)],
        scratch_shapes=[pltpu.VMEM((CHUNKS_PER_STEP, XBC_DIM // LANES, PAD_ROWS, LANES), F32),
                        pltpu.VMEM((SSD_STATE, SSD_DIM), F32),
                        pltpu.VMEM((nc, SSD_STATE, SSD_DIM), F32),
                        pltpu.VMEM((nc, SUBLANES, SSD_DIM), F32)],
        compiler_params=_params(("arbitrary", "arbitrary")),
        name="ssd_scan",
    )(*args)


def _mixer_kernel(u_ref, v_ref, z_ref, act_ref, dt_ref, cum_ref, pf_ref, pb_ref,
                  x_ref, gate_ref, wout_ref, ef_ref, eb_ref,
                  dskip_ref, ng_ref, lg_ref, lb_ref, ws_ref, bs_ref,
                  o_ref, mix_ref, stage_ref):
    @pl.when(pl.program_id(0) == 0)
    def _():
        mix_ref[...] = jnp.zeros_like(mix_ref)

    def proj_piece(n):
        def run():
            cols = slice(n * PROJ_COLS, (n + 1) * PROJ_COLS)
            f = _dot(mix_ref[...], wout_ref[:, cols])
            o_ref[:, cols] = x_ref[:, cols] + gate_ref[:, cols] * f
        return run

    pieces = [proj_piece(n) for n in range(D_MODEL // PROJ_COLS)]
    per_chunk = len(pieces) // CHUNKS_PER_STEP
    for sub in range(CHUNKS_PER_STEP):
        mine = pieces[sub * per_chunk:(sub + 1) * per_chunk]
        _mixer_chunk(sub, u_ref, v_ref, z_ref, act_ref, dt_ref, cum_ref, pf_ref, pb_ref, ef_ref,
                     eb_ref, dskip_ref, ng_ref, lg_ref, lb_ref, ws_ref, bs_ref, stage_ref, mine)
        while mine:
            mine.pop(0)()
    mix_ref[...] = stage_ref[...]


def _mixer_chunk(sub, u_ref, v_ref, z_ref, act_ref, dt_ref, cum_ref, pf_ref, pb_ref, ef_ref, eb_ref,
                 dskip_ref, ng_ref, lg_ref, lb_ref, ws_ref, bs_ref, o_ref, fillers):
    def fill():
        if fillers:
            fillers.pop(0)()

    rows = slice(sub * CHUNK, (sub + 1) * CHUNK)
    xs = act_ref[rows, 0:SSD_DIM]
    gn = SSD_GROUPS * SSD_STATE
    bm = act_ref[rows, SSD_DIM:SSD_DIM + gn].astype(BF16)
    cm = act_ref[rows, SSD_DIM + gn:XBC_DIM].astype(BF16)

    dt = dt_ref[rows, :]
    cum = cum_ref[rows, :]
    ecum = jnp.exp(cum)
    ecum_f = _expand(ecum, ef_ref[...])
    ecum_b = _expand(ecum, eb_ref[...])
    dt_t = dt.T
    cum_t = cum.T

    lane = lax.broadcasted_iota(jnp.int32, (CHUNK, SSD_DIM), 1)
    first_head = (lane % LANES) < SSD_HEAD_DIM
    xs_even = jnp.where(first_head, xs, 0.0).astype(BF16)
    xs_odd = jnp.where(first_head, 0.0, xs).astype(BF16)

    row = lax.broadcasted_iota(jnp.int32, (CHUNK, CHUNK), 0)
    col = lax.broadcasted_iota(jnp.int32, (CHUNK, CHUNK), 1)
    causal = row >= col
    anticausal = row <= col

    def decay_matrix(h):
        lf = jnp.exp(jnp.minimum(cum[:, h:h + 1] - cum_t[h:h + 1, :], 0.0))
        lf = jnp.where(causal, lf, 0.0) * dt_t[h:h + 1, :]
        hb = SSD_HEADS + h
        lb = jnp.exp(jnp.minimum(cum[:, hb:hb + 1] - cum_t[hb:hb + 1, :], 0.0))
        lb = jnp.where(anticausal, lb, 0.0) * dt_t[hb:hb + 1, :]
        return lf + lb

    pf = pf_ref[sub]
    pb = pb_ref[sub]
    heads_per_group = SSD_HEADS // SSD_GROUPS
    y_parts = []
    for g in range(SSD_GROUPS):
        cg = cm[:, g * SSD_STATE:(g + 1) * SSD_STATE]
        bg = bm[:, g * SSD_STATE:(g + 1) * SSD_STATE]
        cb_g = lax.dot_general(cg, bg, (((1,), (1,)), ((), ())), preferred_element_type=F32)
        gs = slice(g * GROUP_DIM, (g + 1) * GROUP_DIM)
        y_off = ecum_f[:, gs] * _dot(cg, pf[:, gs]) + ecum_b[:, gs] * _dot(cg, pb[:, gs])
        diag = []
        for j in range(heads_per_group // 2):
            if j == heads_per_group // 4:
                fill()
            h0 = g * heads_per_group + 2 * j
            m_pair = jnp.concatenate([(cb_g * decay_matrix(h0)).astype(BF16),
                                      (cb_g * decay_matrix(h0 + 1)).astype(BF16)], axis=1)
            ls = slice(h0 * SSD_HEAD_DIM, (h0 + 2) * SSD_HEAD_DIM)
            rhs = jnp.concatenate([xs_even[:, ls], xs_odd[:, ls]], axis=0)
            diag.append(_dot(m_pair, rhs))
        y_parts.append(jnp.concatenate(diag, axis=1) + y_off)
    y = jnp.concatenate(y_parts, axis=1) + dskip_ref[...] * xs

    yz = y * _silu(z_ref[rows, :])
    normed = []
    for g in range(SSD_GROUPS):
        part = yz[:, g * GROUP_DIM:(g + 1) * GROUP_DIM]
        normed.append(part * lax.rsqrt(jnp.mean(part * part, axis=-1, keepdims=True) + EPS))
    ssd_out = jnp.concatenate(normed, axis=1) * ng_ref[...]
    o_ref[rows, A_DIM:A_DIM + SSD_DIM] = ssd_out.astype(BF16)

    fill()
    v = _gelu(v_ref[rows, :])
    vc = v - jnp.mean(v, axis=-1, keepdims=True)
    vn = vc * lax.rsqrt(jnp.mean(vc * vc, axis=-1, keepdims=True) + EPS) * lg_ref[...] + lb_ref[...]
    vn = vn.astype(BF16)
    for h in range(A_HEADS):
        if h == A_HEADS // 2:
            fill()
        hs = slice(h * A_HEAD_DIM, (h + 1) * A_HEAD_DIM)
        s = _dot(ws_ref[h], vn[:, hs]) + bs_ref[:, hs]
        o_ref[rows, hs] = (_gelu(u_ref[rows, hs]) * s).astype(BF16)


def _mixer(p, act, dt, cum, prev_f, prev_b, x2d, gate, w_out, expanders, extra, *, nb, nc):
    steps_per_batch = nc // CHUNKS_PER_STEP
    n_blocks = nb * steps_per_batch
    cur = lambda t: jnp.minimum(t, n_blocks - 1)
    lag = lambda t: jnp.maximum(t - 1, 0)
    lane_blk = lambda k: pl.BlockSpec((STEP_ROWS, A_DIM), lambda t: (cur(t), k))
    cur_rows = lambda width: pl.BlockSpec((STEP_ROWS, width), lambda t: (cur(t), 0))
    lag_rows = pl.BlockSpec((STEP_ROWS, D_MODEL), lambda t: (lag(t), 0))
    lag_mod = pl.BlockSpec((None, 1, D_MODEL), lambda t: (lag(t) // steps_per_batch, 0, 0))
    state_spec = pl.BlockSpec((CHUNKS_PER_STEP, SSD_STATE, SSD_DIM), lambda t: (cur(t), 0, 0))
    whole = lambda a: pl.BlockSpec(a.shape, lambda t: (0,) * a.ndim)
    params = tuple(expanders) + tuple(extra)
    args = (p, p, p, act, dt, cum, prev_f, prev_b, x2d, gate, w_out) + params
    in_specs = [lane_blk(0), lane_blk(1), lane_blk(2), cur_rows(XBC_DIM), cur_rows(LANES),
                cur_rows(LANES), state_spec, state_spec, lag_rows, lag_mod, whole(w_out)]
    in_specs += [whole(a) for a in params]
    t_rows = n_blocks * STEP_ROWS
    return pl.pallas_call(
        _mixer_kernel,
        grid=(n_blocks + 1,),
        in_specs=in_specs,
        out_specs=lag_rows,
        out_shape=jax.ShapeDtypeStruct((t_rows, D_MODEL), F32),
        scratch_shapes=[pltpu.VMEM((STEP_ROWS, A_DIM + SSD_DIM), BF16),
                        pltpu.VMEM((STEP_ROWS, A_DIM + SSD_DIM), BF16)],
        compiler_params=_params(("arbitrary",)),
        name="mixer",
    )(*args)


def _ffn_weights(w_gate, w_up, w_down):
    return w_gate.astype(BF16), w_up.astype(BF16), w_down.astype(BF16)


def _ssd_consts(conv_w, conv_b, dt_bias, a_log):
    lanes_pad = LANES - 2 * SSD_HEADS
    bias_row = jnp.pad(dt_bias.reshape(1, 2 * SSD_HEADS), ((0, 0), (0, lanes_pad)))
    a_row = jnp.pad(-jnp.exp(a_log.reshape(1, 2 * SSD_HEADS)), ((0, 0), (0, lanes_pad)))
    tl = jnp.asarray(np.tril(np.ones((CHUNK, CHUNK), np.float32)))
    head_of_lane = np.arange(SSD_DIM) // SSD_HEAD_DIM
    sel = np.arange(LANES)[:, None] == head_of_lane[None, :]
    sel_b = np.arange(LANES)[:, None] == (head_of_lane[None, :] + SSD_HEADS)
    ef = jnp.asarray(sel.astype(np.float32)).astype(BF16)
    eb = jnp.asarray(sel_b.astype(np.float32)).astype(BF16)
    return (conv_w, conv_b.reshape(1, XBC_DIM), bias_row, a_row, tl, tl.T, ef, eb)


def kernel(x, c, ctx, c_ctx, w_mod, b_mod, norm_ffn1, ffn1_w_gate, ffn1_w_up, ffn1_w_down,
           norm_mix, w_in, conv_w, conv_b, dt_bias, a_log, d_skip, ssd_norm_g, gmlp_norm_g,
           gmlp_norm_b, gmlp_w_s, gmlp_b_s, w_out, norm_ffn2, ffn2_w_gate, ffn2_w_up,
           ffn2_w_down, norm_final):
    assert w_mod.shape[0] == 1, "single-layer block"
    bsz, seq, _ = x.shape
    ctx_len = ctx.shape[1]
    nc_x = seq // CHUNK
    nc_c = ctx_len // CHUNK

    pad_rows = (-(bsz + 1)) % SUBLANES
    c_all = jnp.concatenate([c, c_ctx[None, :], jnp.zeros((pad_rows, D_MODEL), F32)], axis=0)
    mod = _modulation(c_all, w_mod[0], b_mod[0]).reshape(-1, N_MOD, D_MODEL)
    mod_x = [mod[:bsz, k][:, None, :] for k in range(N_MOD)]
    mod_c = [mod[bsz:bsz + 1, k][:, None, :] for k in range(N_MOD)]

    row = lambda a: a.reshape(1, -1)
    x2d = x.reshape(bsz * seq, D_MODEL)
    c2d = ctx.reshape(bsz * ctx_len, D_MODEL)

    w1 = _ffn_weights(ffn1_w_gate[0], ffn1_w_up[0], ffn1_w_down[0])
    g1 = row(norm_ffn1[0])
    x1 = _ffn(x2d, mod_x[0], mod_x[1], mod_x[2], g1, *w1, g1, rows_per_mod=seq,
              final_norm=False)
    c1 = _ffn(c2d, mod_c[0], mod_c[1], mod_c[2], g1, *w1, g1, rows_per_mod=bsz * ctx_len,
              final_norm=False)

    w_proj = w_in[0].astype(BF16)
    w_dt = jnp.pad(w_proj[:, MAIN_COLS:], ((0, 0), (0, LANES - 2 * SSD_HEADS)))
    gm = row(norm_mix[0])
    main_blocks = MAIN_COLS // XBC_DIM
    px, dtx = _inproj(x1, mod_x[3], mod_x[4], gm, w_proj, w_dt, rows_per_mod=seq, first_block=0,
                      n_blocks=main_blocks)
    pc, dtc = _inproj(c1, mod_c[3], mod_c[4], gm, w_proj, w_dt, rows_per_mod=bsz * ctx_len,
                      first_block=main_blocks - 1, n_blocks=1)

    consts = _ssd_consts(conv_w[0], conv_b[0], dt_bias[0], a_log[0])
    h0 = jnp.zeros((bsz, SSD_STATE, SSD_DIM), F32)
    *_, hf, hb = _scan(pc, dtc, h0, h0, consts, nb=bsz, nc=nc_c, col=0)
    act, dts, cum, prev_f, prev_b, _, _ = _scan(px, dtx, hf, hb, consts, nb=bsz, nc=nc_x,
                                                col=(MAIN_COLS - XBC_DIM) // XBC_DIM)

    extra = (row(jnp.repeat(d_skip[0], SSD_HEAD_DIM)), row(ssd_norm_g[0]), row(gmlp_norm_g[0]),
             row(gmlp_norm_b[0]), gmlp_w_s[0].astype(BF16),
             jnp.repeat(gmlp_b_s[0].T, A_HEAD_DIM, axis=1))
    x2 = _mixer(px, act, dts, cum, prev_f, prev_b, x1, mod_x[5], w_out[0].astype(BF16), consts[-2:],
                extra, nb=bsz, nc=nc_x)

    w2 = _ffn_weights(ffn2_w_gate[0], ffn2_w_up[0], ffn2_w_down[0])
    out = _ffn(x2, mod_x[6], mod_x[7], mod_x[8], row(norm_ffn2[0]), *w2, row(norm_final),
               rows_per_mod=seq, final_norm=True)
    return out.reshape(bsz, seq, D_MODEL)
```

```python
import functools
import math

import jax
import jax.numpy as jnp
import numpy as np
from jax import lax
from jax.experimental import pallas as pl
from jax.experimental.pallas import tpu as pltpu

F32 = jnp.float32
BF16 = jnp.bfloat16

D_MODEL = 2048
N_MOD = 9
D_FF = 5504
A_HEADS = 8
A_HEAD_DIM = 128
A_DIM = A_HEADS * A_HEAD_DIM
SSD_HEADS = 16
SSD_HEAD_DIM = 64
SSD_DIM = SSD_HEADS * SSD_HEAD_DIM
SSD_GROUPS = 2
SSD_STATE = 128
GROUP_DIM = SSD_DIM // SSD_GROUPS
CHUNK = 128
CONV_W = 5
XBC_DIM = SSD_DIM + 2 * SSD_GROUPS * SSD_STATE
MAIN_COLS = 2 * A_DIM + SSD_DIM + XBC_DIM
EPS = 1e-6

LANES = 128
SUBLANES = 8
VMEM_LIMIT_BYTES = 62 * 1024 * 1024

HALO = SUBLANES
PAD_ROWS = CHUNK + 2 * HALO


def _params(sem):
    return pltpu.CompilerParams(dimension_semantics=sem, vmem_limit_bytes=VMEM_LIMIT_BYTES)


def _dot(a, b):
    return jnp.dot(a, b, preferred_element_type=F32)


def _silu(x):
    return x * jax.nn.sigmoid(x)


def _gelu(x):
    return 0.5 * x * (1.0 + lax.erf(x * (1.0 / math.sqrt(2.0))))


def _softplus(x):
    return jnp.maximum(x, 0.0) + jnp.log1p(jnp.exp(-jnp.abs(x)))


def _rms(x, g):
    return x * lax.rsqrt(jnp.mean(x * x, axis=-1, keepdims=True) + EPS) * g


def _mod_kernel(c_ref, w_ref, b_ref, o_ref):
    o_ref[...] = _dot(_silu(c_ref[...]), w_ref[...]) + b_ref[...]


def _modulation(c_all, w_mod, b_mod, tn=1024):
    rows = c_all.shape[0]
    n = w_mod.shape[1]
    return pl.pallas_call(
        _mod_kernel,
        grid=(n // tn,),
        in_specs=[pl.BlockSpec((rows, D_MODEL), lambda j: (0, 0)),
                  pl.BlockSpec((D_MODEL, tn), lambda j: (0, j)),
                  pl.BlockSpec((1, tn), lambda j: (0, j))],
        out_specs=pl.BlockSpec((rows, tn), lambda j: (0, j)),
        out_shape=jax.ShapeDtypeStruct((rows, n), F32),
        compiler_params=_params(("arbitrary",)),
        name="mod",
    )(c_all, w_mod, b_mod.reshape(1, n))


FFN_TM = 1024
FFN_TF = 768
FFN_TF_FINAL = 512


def _norm_mod(x, g, shift, scale):
    return (_rms(x, g) * (1.0 + scale) + shift).astype(BF16)


def _ffn_kernel(x_ref, shift_ref, scale_ref, gate_ref, g_ref, wg_ref, wu_ref, wd_ref, gf_ref,
                o_ref, h_ref, *, final_norm, tf, last_cols):
    j = pl.program_id(1)
    last = pl.num_programs(1) - 1

    @pl.when(j == 0)
    def _():
        h_ref[...] = _norm_mod(x_ref[...], g_ref[...], shift_ref[...], scale_ref[...])

    def partial_ffn(cols):
        h = h_ref[...]
        gate = _dot(h, wg_ref[:, 0:cols])
        up = _dot(h, wu_ref[:, 0:cols])
        a = (_silu(gate) * up).astype(BF16)
        return _dot(a, wd_ref[0:cols, :])

    @pl.when(j == 0)
    def _():
        o_ref[...] = partial_ffn(tf)

    @pl.when((j > 0) & (j < last))
    def _():
        o_ref[...] += partial_ffn(tf)

    @pl.when(j == last)
    def _():
        f = o_ref[...] + partial_ffn(last_cols)
        r = x_ref[...] + (0.5 * gate_ref[...]) * f
        if final_norm:
            r = _rms(r, gf_ref[...])
        o_ref[...] = r


def _ffn(x2d, shift, scale, gate, g, wg, wu, wd, gf, *, rows_per_mod, final_norm):
    t = x2d.shape[0]
    tm = min(FFN_TM, rows_per_mod)
    tf = FFN_TF_FINAL if final_norm else FFN_TF
    nf = pl.cdiv(D_FF, tf)
    assert nf >= 3 and t % tm == 0 and rows_per_mod % tm == 0
    tiles_per_mod = rows_per_mod // tm
    mod_spec = pl.BlockSpec((None, 1, D_MODEL), lambda i, j: (i // tiles_per_mod, 0, 0))
    row_spec = pl.BlockSpec((1, D_MODEL), lambda i, j: (0, 0))
    tile_spec = pl.BlockSpec((tm, D_MODEL), lambda i, j: (i, 0))
    w_specs = [pl.BlockSpec((D_MODEL, tf), lambda i, j: (0, j)),
               pl.BlockSpec((D_MODEL, tf), lambda i, j: (0, j)),
               pl.BlockSpec((tf, D_MODEL), lambda i, j: (j, 0))]
    args = (x2d, shift, scale, gate, g, wg, wu, wd, gf)
    in_specs = [tile_spec, mod_spec, mod_spec, mod_spec, row_spec] + w_specs + [row_spec]
    return pl.pallas_call(
        functools.partial(_ffn_kernel, final_norm=final_norm, tf=tf,
                          last_cols=D_FF - (nf - 1) * tf),
        grid=(t // tm, nf),
        in_specs=in_specs,
        out_specs=tile_spec,
        out_shape=jax.ShapeDtypeStruct((t, D_MODEL), F32),
        scratch_shapes=[pltpu.VMEM((tm, D_MODEL), BF16)],
        compiler_params=_params(("arbitrary", "arbitrary")),
        name="ffn_final" if final_norm else "ffn",
    )(*args)


def _inproj_kernel(x_ref, shift_ref, scale_ref, g_ref, w_ref, wdt_ref, p_ref, dt_ref, h_ref):
    @pl.when(pl.program_id(1) == 0)
    def _():
        h_ref[...] = _norm_mod(x_ref[...], g_ref[...], shift_ref[...], scale_ref[...])
        p_ref[...] = _dot(h_ref[...], w_ref[...])
        dt_ref[...] = _dot(h_ref[...], wdt_ref[...])

    @pl.when(pl.program_id(1) > 0)
    def _():
        p_ref[...] = _dot(h_ref[...], w_ref[...])


def _inproj(x2d, shift, scale, g, w, w_dt, *, rows_per_mod, first_block, n_blocks, tn=XBC_DIM):
    t = x2d.shape[0]
    tm = min(1024, rows_per_mod)
    tiles_per_mod = rows_per_mod // tm
    mod_spec = pl.BlockSpec((None, 1, D_MODEL), lambda i, j: (i // tiles_per_mod, 0, 0))
    return pl.pallas_call(
        _inproj_kernel,
        grid=(t // tm, n_blocks),
        in_specs=[pl.BlockSpec((tm, D_MODEL), lambda i, j: (i, 0)), mod_spec, mod_spec,
                  pl.BlockSpec((1, D_MODEL), lambda i, j: (0, 0)),
                  pl.BlockSpec((D_MODEL, tn), lambda i, j: (0, first_block + j)),
                  pl.BlockSpec((D_MODEL, LANES), lambda i, j: (0, 0))],
        out_specs=[pl.BlockSpec((tm, tn), lambda i, j: (i, j)),
                   pl.BlockSpec((tm, LANES), lambda i, j: (i, 0))],
        out_shape=[jax.ShapeDtypeStruct((t, n_blocks * tn), F32),
                   jax.ShapeDtypeStruct((t, LANES), F32)],
        scratch_shapes=[pltpu.VMEM((tm, D_MODEL), BF16)],
        compiler_params=_params(("arbitrary", "arbitrary")),
        name="inproj",
    )(x2d, shift, scale, g, w, w_dt)


def _conv_silu(cur_ref, r0, prev_rows, next_rows, w_ref, b_ref, pad_ref, sub):
    outs = []
    for s in range(XBC_DIM // LANES):
        ls = slice(s * LANES, (s + 1) * LANES)
        pad_ref[sub, s, 0:HALO, :] = prev_rows(ls)
        pad_ref[sub, s, HALO:HALO + CHUNK, :] = cur_ref[r0:r0 + CHUNK, ls]
        pad_ref[sub, s, HALO + CHUNK:PAD_ROWS, :] = next_rows(ls)
        acc = jnp.broadcast_to(b_ref[:, ls], (CHUNK, LANES))
        for k in range(CONV_W):
            start = HALO - CONV_W // 2 + k
            acc = acc + pad_ref[sub, s, pl.ds(start, CHUNK, stride=1), :] * w_ref[k:k + 1, ls]
        outs.append(_silu(acc))
    return jnp.concatenate(outs, axis=1)


def _dt_terms(dt_raw, bias_row, a_row, tl, tu):
    dt = _softplus(dt_raw + bias_row)
    a = dt * a_row
    s = jnp.dot(tl, a, precision=lax.Precision.HIGHEST, preferred_element_type=F32)
    r = jnp.dot(tu, a, precision=lax.Precision.HIGHEST, preferred_element_type=F32)
    is_fwd = lax.broadcasted_iota(jnp.int32, (CHUNK, LANES), 1) < SSD_HEADS
    cum = jnp.where(is_fwd, s, r)
    total = jnp.where(is_fwd[0:1], s[CHUNK - 1:CHUNK], r[0:1])
    return dt, cum, total


def _expand(v, e):
    hi = v.astype(BF16)
    lo = (v - hi.astype(F32)).astype(BF16)
    return _dot(hi, e) + _dot(lo, e)


PROJ_COLS = 256
CHUNKS_PER_STEP = 2
STEP_ROWS = CHUNKS_PER_STEP * CHUNK


def _scan_kernel(xc_ref, xp_ref, xn_ref, dt_ref, h0f_ref, h0b_ref, cw_ref, cb_ref, bias_ref,
                 arow_ref, tl_ref, tu_ref, ef_ref, eb_ref, act_ref, dto_ref, cum_ref, pf_ref, pb_ref,
                 ff_ref, fb_ref, pad_ref, sf_ref, sball_ref, dball_ref, *, nc):
    c = pl.program_id(1)
    nsteps = nc // CHUNKS_PER_STEP

    @pl.when(c == 0)
    def _():
        sf_ref[...] = h0f_ref[...]

    expand_in = []
    for sub in range(CHUNKS_PER_STEP):
        r0 = sub * CHUNK
        dt, cum, total = _dt_terms(dt_ref[r0:r0 + CHUNK, :], bias_ref[...], arow_ref[...],
                                   tl_ref[...], tu_ref[...])
        dto_ref[r0:r0 + CHUNK, :] = dt
        cum_ref[r0:r0 + CHUNK, :] = cum
        w = dt * jnp.exp(total - cum)
        decay = jnp.broadcast_to(jnp.exp(total), (SUBLANES, LANES))
        expand_in.append(jnp.concatenate([w, decay], axis=0))

    for sub in range(CHUNKS_PER_STEP):
        r0 = sub * CHUNK
        if sub == 0:
            prev_rows = lambda ls: jnp.where(c == 0, 0.0, xp_ref[:, ls])
        else:
            prev_rows = lambda ls, r0=r0: xc_ref[r0 - HALO:r0, ls]
        if sub == CHUNKS_PER_STEP - 1:
            next_rows = lambda ls: jnp.where(c == nsteps - 1, 0.0, xn_ref[:, ls])
        else:
            next_rows = lambda ls, r0=r0: xc_ref[r0 + CHUNK:r0 + CHUNK + HALO, ls]
        act_ref[r0:r0 + CHUNK, :] = _conv_silu(xc_ref, r0, prev_rows, next_rows, cw_ref, cb_ref,
                                               pad_ref, sub)

    state = sf_ref[...]
    for sub in range(CHUNKS_PER_STEP):
        r0 = sub * CHUNK
        v = expand_in[sub]
        xs = act_ref[r0:r0 + CHUNK, 0:SSD_DIM]
        bt = []
        for g in range(SSD_GROUPS):
            lo = SSD_DIM + g * SSD_STATE
            bt.append(act_ref[r0:r0 + CHUNK, lo:lo + SSD_STATE].T.astype(BF16))

        def chunk_state(e_ref):
            ex = _expand(v, e_ref[...])
            xw = (xs * ex[0:CHUNK]).astype(BF16)
            new = [_dot(bt[g], xw[:, g * GROUP_DIM:(g + 1) * GROUP_DIM])
                   for g in range(SSD_GROUPS)]
            return jnp.concatenate(new, axis=1), ex[CHUNK:CHUNK + SUBLANES]

        new_f, decay_f = chunk_state(ef_ref)
        pf_ref[sub] = state.astype(BF16)
        state = state * decay_f[0:1] + new_f

        new_b, decay_b = chunk_state(eb_ref)
        sball_ref[c * CHUNKS_PER_STEP + sub] = new_b
        dball_ref[c * CHUNKS_PER_STEP + sub] = decay_b
    sf_ref[...] = state

    @pl.when(c == nsteps - 1)
    def _():
        ff_ref[...] = sf_ref[...]
        sb = h0b_ref[...]
        for cc in reversed(range(nc)):
            pb_ref[cc] = sb.astype(BF16)
            sb = sb * dball_ref[cc, 0:1, :] + sball_ref[cc]
        fb_ref[...] = sb


def _halo_specs(nsteps, col):
    per = STEP_ROWS // HALO
    cur = pl.BlockSpec((STEP_ROWS, XBC_DIM), lambda b, c: (b * nsteps + c, col))
    prev = pl.BlockSpec((HALO, XBC_DIM),
                        lambda b, c: (jnp.maximum((b * nsteps + c) * per - 1, 0), col))
    nxt = pl.BlockSpec((HALO, XBC_DIM),
                       lambda b, c: (jnp.minimum((b * nsteps + c + 1) * per,
                                                 (b * nsteps + nsteps) * per - 1), col))
    return cur, prev, nxt


def _full_spec(a):
    return pl.BlockSpec(a.shape, lambda b, c: (0,) * a.ndim)


def _scan(p, dt, h0f, h0b, consts, *, nb, nc, col):
    assert nc % CHUNKS_PER_STEP == 0
    nsteps = nc // CHUNKS_PER_STEP
    state_spec = pl.BlockSpec((None, SSD_STATE, SSD_DIM), lambda b, c: (b, 0, 0))
    step_rows = lambda width: pl.BlockSpec((STEP_ROWS, width), lambda b, c: (b * nsteps + c, 0))
    args = (p, p, p, dt, h0f, h0b) + tuple(consts)
    in_specs = list(_halo_specs(nsteps, col)) + [step_rows(LANES), state_spec, state_spec]
    in_specs += [_full_spec(a) for a in consts]
    return pl.pallas_call(
        functools.partial(_scan_kernel, nc=nc),
        grid=(nb, nsteps),
        in_specs=in_specs,
        out_specs=[step_rows(XBC_DIM), step_rows(LANES), step_rows(LANES),
                   pl.BlockSpec((CHUNKS_PER_STEP, SSD_STATE, SSD_DIM),
                                lambda b, c: (b * nsteps + c, 0, 0)),
                   pl.BlockSpec((nc, SSD_STATE, SSD_DIM), lambda b, c: (b, 0, 0)),
                   state_spec, state_spec],
        out_shape=[jax.ShapeDtypeStruct((nb * nc * CHUNK, XBC_DIM), F32),
                   jax.ShapeDtypeStruct((nb * nc * CHUNK, LANES), F32),
                   jax.ShapeDtypeStruct((nb * nc * CHUNK, LANES), F32),
                   jax.ShapeDtypeStruct((nb * nc, SSD_STATE, SSD_DIM), BF16),
                   jax.ShapeDtypeStruct((nb * nc, SSD_STATE, SSD_DIM), BF16),
                   jax.ShapeDtypeStruct((nb, SSD_STATE, SSD_DIM), F32),
                   jax.ShapeDtypeStruct((nb, SSD_STATE, SSD_DIM), F32)],
        scratch_shapes=[pltpu.VMEM((CHUNKS_PER_STEP, XBC_DIM // LANES, PAD_ROWS, LANES), F32),
                        pltpu.VMEM((SSD_STATE, SSD_DIM), F32),
                        pltpu.VMEM((nc, SSD_STATE, SSD_DIM), F32),
                        pltpu.VMEM((nc, SUBLANES, SSD_DIM), F32)],
        compiler_params=_params(("arbitrary", "arbitrary")),
        name="ssd_scan",
    )(*args)


def _mixer_kernel(u_ref, v_ref, z_ref, act_ref, dt_ref, cum_ref, pf_ref, pb_ref,
                  x_ref, gate_ref, wout_ref, ef_ref, eb_ref,
                  dskip_ref, ng_ref, lg_ref, lb_ref, ws_ref, bs_ref,
                  o_ref, mix_ref, stage_ref):
    @pl.when(pl.program_id(0) == 0)
    def _():
        mix_ref[...] = jnp.zeros_like(mix_ref)

    def proj_piece(n):
        def run():
            cols = slice(n * PROJ_COLS, (n + 1) * PROJ_COLS)
            f = _dot(mix_ref[...], wout_ref[:, cols])
            o_ref[:, cols] = x_ref[:, cols] + gate_ref[:, cols] * f
        return run

    pieces = [proj_piece(n) for n in range(D_MODEL // PROJ_COLS)]
    per_chunk = len(pieces) // CHUNKS_PER_STEP
    for sub in range(CHUNKS_PER_STEP):
        mine = pieces[sub * per_chunk:(sub + 1) * per_chunk]
        _mixer_chunk(sub, u_ref, v_ref, z_ref, act_ref, dt_ref, cum_ref, pf_ref, pb_ref, ef_ref,
                     eb_ref, dskip_ref, ng_ref, lg_ref, lb_ref, ws_ref, bs_ref, stage_ref, mine)
        while mine:
            mine.pop(0)()
    mix_ref[...] = stage_ref[...]


def _mixer_chunk(sub, u_ref, v_ref, z_ref, act_ref, dt_ref, cum_ref, pf_ref, pb_ref, ef_ref, eb_ref,
                 dskip_ref, ng_ref, lg_ref, lb_ref, ws_ref, bs_ref, o_ref, fillers):
    def fill():
        if fillers:
            fillers.pop(0)()

    rows = slice(sub * CHUNK, (sub + 1) * CHUNK)
    xs = act_ref[rows, 0:SSD_DIM]
    gn = SSD_GROUPS * SSD_STATE
    bm = act_ref[rows, SSD_DIM:SSD_DIM + gn].astype(BF16)
    cm = act_ref[rows, SSD_DIM + gn:XBC_DIM].astype(BF16)

    dt = dt_ref[rows, :]
    cum = cum_ref[rows, :]
    ecum = jnp.exp(cum)
    ecum_f = _expand(ecum, ef_ref[...])
    ecum_b = _expand(ecum, eb_ref[...])
    dt_t = dt.T
    cum_t = cum.T

    lane = lax.broadcasted_iota(jnp.int32, (CHUNK, SSD_DIM), 1)
    first_head = (lane % LANES) < SSD_HEAD_DIM
    xs_even = jnp.where(first_head, xs, 0.0).astype(BF16)
    xs_odd = jnp.where(first_head, 0.0, xs).astype(BF16)

    row = lax.broadcasted_iota(jnp.int32, (CHUNK, CHUNK), 0)
    col = lax.broadcasted_iota(jnp.int32, (CHUNK, CHUNK), 1)
    causal = row >= col
    anticausal = row <= col

    def decay_matrix(h):
        lf = jnp.exp(jnp.minimum(cum[:, h:h + 1] - cum_t[h:h + 1, :], 0.0))
        lf = jnp.where(causal, lf, 0.0) * dt_t[h:h + 1, :]
        hb = SSD_HEADS + h
        lb = jnp.exp(jnp.minimum(cum[:, hb:hb + 1] - cum_t[hb:hb + 1, :], 0.0))
        lb = jnp.where(anticausal, lb, 0.0) * dt_t[hb:hb + 1, :]
        return lf + lb

    pf = pf_ref[sub]
    pb = pb_ref[sub]
    heads_per_group = SSD_HEADS // SSD_GROUPS
    y_parts = []
    for g in range(SSD_GROUPS):
        cg = cm[:, g * SSD_STATE:(g + 1) * SSD_STATE]
        bg = bm[:, g * SSD_STATE:(g + 1) * SSD_STATE]
        cb_g = lax.dot_general(cg, bg, (((1,), (1,)), ((), ())), preferred_element_type=F32)
        gs = slice(g * GROUP_DIM, (g + 1) * GROUP_DIM)
        y_off = ecum_f[:, gs] * _dot(cg, pf[:, gs]) + ecum_b[:, gs] * _dot(cg, pb[:, gs])
        diag = []
        for j in range(heads_per_group // 2):
            if j == heads_per_group // 4:
                fill()
            h0 = g * heads_per_group + 2 * j
            m_pair = jnp.concatenate([(cb_g * decay_matrix(h0)).astype(BF16),
                                      (cb_g * decay_matrix(h0 + 1)).astype(BF16)], axis=1)
            ls = slice(h0 * SSD_HEAD_DIM, (h0 + 2) * SSD_HEAD_DIM)
            rhs = jnp.concatenate([xs_even[:, ls], xs_odd[:, ls]], axis=0)
            diag.append(_dot(m_pair, rhs))
        y_parts.append(jnp.concatenate(diag, axis=1) + y_off)
    y = jnp.concatenate(y_parts, axis=1) + dskip_ref[...] * xs

    yz = y * _silu(z_ref[rows, :])
    normed = []
    for g in range(SSD_GROUPS):
        part = yz[:, g * GROUP_DIM:(g + 1) * GROUP_DIM]
        normed.append(part * lax.rsqrt(jnp.mean(part * part, axis=-1, keepdims=True) + EPS))
    ssd_out = jnp.concatenate(normed, axis=1) * ng_ref[...]
    o_ref[rows, A_DIM:A_DIM + SSD_DIM] = ssd_out.astype(BF16)

    fill()
    v = _gelu(v_ref[rows, :])
    vc = v - jnp.mean(v, axis=-1, keepdims=True)
    vn = vc * lax.rsqrt(jnp.mean(vc * vc, axis=-1, keepdims=True) + EPS) * lg_ref[...] + lb_ref[...]
    vn = vn.astype(BF16)
    for h in range(A_HEADS):
        if h == A_HEADS // 2:
            fill()
        hs = slice(h * A_HEAD_DIM, (h + 1) * A_HEAD_DIM)
        s = _dot(ws_ref[h], vn[:, hs]) + bs_ref[:, hs]
        o_ref[rows, hs] = (_gelu(u_ref[rows, hs]) * s).astype(BF16)


def _mixer(p, act, dt, cum, prev_f, prev_b, x2d, gate, w_out, expanders, extra, *, nb, nc):
    steps_per_batch = nc // CHUNKS_PER_STEP
    n_blocks = nb * steps_per_batch
    cur = lambda t: jnp.minimum(t, n_blocks - 1)
    lag = lambda t: jnp.maximum(t - 1, 0)
    lane_blk = lambda k: pl.BlockSpec((STEP_ROWS, A_DIM), lambda t: (cur(t), k))
    cur_rows = lambda width: pl.BlockSpec((STEP_ROWS, width), lambda t: (cur(t), 0))
    lag_rows = pl.BlockSpec((STEP_ROWS, D_MODEL), lambda t: (lag(t), 0))
    lag_mod = pl.BlockSpec((None, 1, D_MODEL), lambda t: (lag(t) // steps_per_batch, 0, 0))
    state_spec = pl.BlockSpec((CHUNKS_PER_STEP, SSD_STATE, SSD_DIM), lambda t: (cur(t), 0, 0))
    whole = lambda a: pl.BlockSpec(a.shape, lambda t: (0,) * a.ndim)
    params = tuple(expanders) + tuple(extra)
    args = (p, p, p, act, dt, cum, prev_f, prev_b, x2d, gate, w_out) + params
    in_specs = [lane_blk(0), lane_blk(1), lane_blk(2), cur_rows(XBC_DIM), cur_rows(LANES),
                cur_rows(LANES), state_spec, state_spec, lag_rows, lag_mod, whole(w_out)]
    in_specs += [whole(a) for a in params]
    t_rows = n_blocks * STEP_ROWS
    return pl.pallas_call(
        _mixer_kernel,
        grid=(n_blocks + 1,),
        in_specs=in_specs,
        out_specs=lag_rows,
        out_shape=jax.ShapeDtypeStruct((t_rows, D_MODEL), F32),
        scratch_shapes=[pltpu.VMEM((STEP_ROWS, A_DIM + SSD_DIM), BF16),
                        pltpu.VMEM((STEP_ROWS, A_DIM + SSD_DIM), BF16)],
        compiler_params=_params(("arbitrary",)),
        name="mixer",
    )(*args)


def _ffn_weights(w_gate, w_up, w_down):
    return w_gate.astype(BF16), w_up.astype(BF16), w_down.astype(BF16)


def _ssd_consts(conv_w, conv_b, dt_bias, a_log):
    lanes_pad = LANES - 2 * SSD_HEADS
    bias_row = jnp.pad(dt_bias.reshape(1, 2 * SSD_HEADS), ((0, 0), (0, lanes_pad)))
    a_row = jnp.pad(-jnp.exp(a_log.reshape(1, 2 * SSD_HEADS)), ((0, 0), (0, lanes_pad)))
    tl = jnp.asarray(np.tril(np.ones((CHUNK, CHUNK), np.float32)))
    head_of_lane = np.arange(SSD_DIM) // SSD_HEAD_DIM
    sel = np.arange(LANES)[:, None] == head_of_lane[None, :]
    sel_b = np.arange(LANES)[:, None] == (head_of_lane[None, :] + SSD_HEADS)
    ef = jnp.asarray(sel.astype(np.float32)).astype(BF16)
    eb = jnp.asarray(sel_b.astype(np.float32)).astype(BF16)
    return (conv_w, conv_b.reshape(1, XBC_DIM), bias_row, a_row, tl, tl.T, ef, eb)


def kernel(x, c, ctx, c_ctx, w_mod, b_mod, norm_ffn1, ffn1_w_gate, ffn1_w_up, ffn1_w_down,
           norm_mix, w_in, conv_w, conv_b, dt_bias, a_log, d_skip, ssd_norm_g, gmlp_norm_g,
           gmlp_norm_b, gmlp_w_s, gmlp_b_s, w_out, norm_ffn2, ffn2_w_gate, ffn2_w_up,
           ffn2_w_down, norm_final):
    assert w_mod.shape[0] == 1, "single-layer block"
    bsz, seq, _ = x.shape
    ctx_len = ctx.shape[1]
    nc_x = seq // CHUNK
    nc_c = ctx_len // CHUNK

    pad_rows = (-(bsz + 1)) % SUBLANES
    c_all = jnp.concatenate([c, c_ctx[None, :], jnp.zeros((pad_rows, D_MODEL), F32)], axis=0)
    mod = _modulation(c_all, w_mod[0], b_mod[0]).reshape(-1, N_MOD, D_MODEL)
    mod_x = [mod[:bsz, k][:, None, :] for k in range(N_MOD)]
    mod_c = [mod[bsz:bsz + 1, k][:, None, :] for k in range(N_MOD)]

    row = lambda a: a.reshape(1, -1)
    x2d = x.reshape(bsz * seq, D_MODEL)
    c2d = ctx.reshape(bsz * ctx_len, D_MODEL)

    w1 = _ffn_weights(ffn1_w_gate[0], ffn1_w_up[0], ffn1_w_down[0])
    g1 = row(norm_ffn1[0])
    x1 = _ffn(x2d, mod_x[0], mod_x[1], mod_x[2], g1, *w1, g1, rows_per_mod=seq,
              final_norm=False)
    c1 = _ffn(c2d, mod_c[0], mod_c[1], mod_c[2], g1, *w1, g1, rows_per_mod=bsz * ctx_len,
              final_norm=False)

    w_proj = w_in[0].astype(BF16)
    w_dt = jnp.pad(w_proj[:, MAIN_COLS:], ((0, 0), (0, LANES - 2 * SSD_HEADS)))
    gm = row(norm_mix[0])
    main_blocks = MAIN_COLS // XBC_DIM
    px, dtx = _inproj(x1, mod_x[3], mod_x[4], gm, w_proj, w_dt, rows_per_mod=seq, first_block=0,
                      n_blocks=main_blocks)
    pc, dtc = _inproj(c1, mod_c[3], mod_c[4], gm, w_proj, w_dt, rows_per_mod=bsz * ctx_len,
                      first_block=main_blocks - 1, n_blocks=1)

    consts = _ssd_consts(conv_w[0], conv_b[0], dt_bias[0], a_log[0])
    h0 = jnp.zeros((bsz, SSD_STATE, SSD_DIM), F32)
    *_, hf, hb = _scan(pc, dtc, h0, h0, consts, nb=bsz, nc=nc_c, col=0)
    act, dts, cum, prev_f, prev_b, _, _ = _scan(px, dtx, hf, hb, consts, nb=bsz, nc=nc_x,
                                                col=(MAIN_COLS - XBC_DIM) // XBC_DIM)

    extra = (row(jnp.repeat(d_skip[0], SSD_HEAD_DIM)), row(ssd_norm_g[0]), row(gmlp_norm_g[0]),
             row(gmlp_norm_b[0]), gmlp_w_s[0].astype(BF16),
             jnp.repeat(gmlp_b_s[0].T, A_HEAD_DIM, axis=1))
    x2 = _mixer(px, act, dts, cum, prev_f, prev_b, x1, mod_x[5], w_out[0].astype(BF16), consts[-2:],
                extra, nb=bsz, nc=nc_x)

    w2 = _ffn_weights(ffn2_w_gate[0], ffn2_w_up[0], ffn2_w_down[0])
    out = _ffn(x2, mod_x[6], mod_x[7], mod_x[8], row(norm_ffn2[0]), *w2, row(norm_final),
               rows_per_mod=seq, final_norm=True)
    return out.reshape(bsz, seq, D_MODEL)
```

```python
import functools
import math

import jax
import jax.numpy as jnp
import numpy as np
from jax import lax
from jax.experimental import pallas as pl
from jax.experimental.pallas import tpu as pltpu

F32 = jnp.float32
BF16 = jnp.bfloat16

D_MODEL = 2048
N_MOD = 9
D_FF = 5504
A_HEADS = 8
A_HEAD_DIM = 128
A_DIM = A_HEADS * A_HEAD_DIM
SSD_HEADS = 16
SSD_HEAD_DIM = 64
SSD_DIM = SSD_HEADS * SSD_HEAD_DIM
SSD_GROUPS = 2
SSD_STATE = 128
GROUP_DIM = SSD_DIM // SSD_GROUPS
CHUNK = 128
CONV_W = 5
XBC_DIM = SSD_DIM + 2 * SSD_GROUPS * SSD_STATE
MAIN_COLS = 2 * A_DIM + SSD_DIM + XBC_DIM
EPS = 1e-6

LANES = 128
SUBLANES = 8
VMEM_LIMIT_BYTES = 60 * 1024 * 1024

HALO = SUBLANES
PAD_ROWS = CHUNK + 2 * HALO


def _params(sem):
    return pltpu.CompilerParams(dimension_semantics=sem, vmem_limit_bytes=VMEM_LIMIT_BYTES)


def _dot(a, b):
    return jnp.dot(a, b, preferred_element_type=F32)


def _silu(x):
    return x * jax.nn.sigmoid(x)


def _gelu(x):
    return 0.5 * x * (1.0 + lax.erf(x * (1.0 / math.sqrt(2.0))))


def _softplus(x):
    return jnp.maximum(x, 0.0) + jnp.log1p(jnp.exp(-jnp.abs(x)))


def _rms(x, g):
    return x * lax.rsqrt(jnp.mean(x * x, axis=-1, keepdims=True) + EPS) * g


def _mod_kernel(c_ref, w_ref, b_ref, o_ref):
    o_ref[...] = _dot(_silu(c_ref[...]), w_ref[...]) + b_ref[...]


def _modulation(c_all, w_mod, b_mod, tn=1024):
    rows = c_all.shape[0]
    n = w_mod.shape[1]
    return pl.pallas_call(
        _mod_kernel,
        grid=(n // tn,),
        in_specs=[pl.BlockSpec((rows, D_MODEL), lambda j: (0, 0)),
                  pl.BlockSpec((D_MODEL, tn), lambda j: (0, j)),
                  pl.BlockSpec((1, tn), lambda j: (0, j))],
        out_specs=pl.BlockSpec((rows, tn), lambda j: (0, j)),
        out_shape=jax.ShapeDtypeStruct((rows, n), F32),
        compiler_params=_params(("arbitrary",)),
        name="mod",
    )(c_all, w_mod, b_mod.reshape(1, n))


FFN_TM = 1024
FFN_TF = 512


def _norm_mod(x, g, shift, scale):
    return (_rms(x, g) * (1.0 + scale) + shift).astype(BF16)


def _ffn_kernel(x_ref, shift_ref, scale_ref, gate_ref, g_ref, wg_ref, wu_ref, wd_ref, gf_ref,
                o_ref, h_ref, *, final_norm, last_cols):
    j = pl.program_id(1)
    last = pl.num_programs(1) - 1

    @pl.when(j == 0)
    def _():
        h_ref[...] = _norm_mod(x_ref[...], g_ref[...], shift_ref[...], scale_ref[...])

    def partial_ffn(cols):
        h = h_ref[...]
        gate = _dot(h, wg_ref[:, 0:cols])
        up = _dot(h, wu_ref[:, 0:cols])
        a = (_silu(gate) * up).astype(BF16)
        return _dot(a, wd_ref[0:cols, :])

    @pl.when(j == 0)
    def _():
        o_ref[...] = partial_ffn(FFN_TF)

    @pl.when((j > 0) & (j < last))
    def _():
        o_ref[...] += partial_ffn(FFN_TF)

    @pl.when(j == last)
    def _():
        f = o_ref[...] + partial_ffn(last_cols)
        r = x_ref[...] + (0.5 * gate_ref[...]) * f
        if final_norm:
            r = _rms(r, gf_ref[...])
        o_ref[...] = r


def _ffn(x2d, shift, scale, gate, g, wg, wu, wd, gf, *, rows_per_mod, final_norm):
    t = x2d.shape[0]
    tm, tf = min(FFN_TM, rows_per_mod), FFN_TF
    nf = pl.cdiv(D_FF, tf)
    assert nf >= 3 and t % tm == 0 and rows_per_mod % tm == 0
    tiles_per_mod = rows_per_mod // tm
    mod_spec = pl.BlockSpec((None, 1, D_MODEL), lambda i, j: (i // tiles_per_mod, 0, 0))
    row_spec = pl.BlockSpec((1, D_MODEL), lambda i, j: (0, 0))
    tile_spec = pl.BlockSpec((tm, D_MODEL), lambda i, j: (i, 0))
    w_specs = [pl.BlockSpec((D_MODEL, tf), lambda i, j: (0, j)),
               pl.BlockSpec((D_MODEL, tf), lambda i, j: (0, j)),
               pl.BlockSpec((tf, D_MODEL), lambda i, j: (j, 0))]
    args = (x2d, shift, scale, gate, g, wg, wu, wd, gf)
    in_specs = [tile_spec, mod_spec, mod_spec, mod_spec, row_spec] + w_specs + [row_spec]
    return pl.pallas_call(
        functools.partial(_ffn_kernel, final_norm=final_norm, last_cols=D_FF - (nf - 1) * tf),
        grid=(t // tm, nf),
        in_specs=in_specs,
        out_specs=tile_spec,
        out_shape=jax.ShapeDtypeStruct((t, D_MODEL), F32),
        scratch_shapes=[pltpu.VMEM((tm, D_MODEL), BF16)],
        compiler_params=_params(("arbitrary", "arbitrary")),
        name="ffn_final" if final_norm else "ffn",
    )(*args)


def _inproj_kernel(x_ref, shift_ref, scale_ref, g_ref, w_ref, wdt_ref, p_ref, dt_ref, h_ref):
    @pl.when(pl.program_id(1) == 0)
    def _():
        h_ref[...] = _norm_mod(x_ref[...], g_ref[...], shift_ref[...], scale_ref[...])
        p_ref[...] = _dot(h_ref[...], w_ref[...])
        dt_ref[...] = _dot(h_ref[...], wdt_ref[...])

    @pl.when(pl.program_id(1) > 0)
    def _():
        p_ref[...] = _dot(h_ref[...], w_ref[...])


def _inproj(x2d, shift, scale, g, w, w_dt, *, rows_per_mod, first_block, n_blocks, tn=XBC_DIM):
    t = x2d.shape[0]
    tm = min(1024, rows_per_mod)
    tiles_per_mod = rows_per_mod // tm
    mod_spec = pl.BlockSpec((None, 1, D_MODEL), lambda i, j: (i // tiles_per_mod, 0, 0))
    return pl.pallas_call(
        _inproj_kernel,
        grid=(t // tm, n_blocks),
        in_specs=[pl.BlockSpec((tm, D_MODEL), lambda i, j: (i, 0)), mod_spec, mod_spec,
                  pl.BlockSpec((1, D_MODEL), lambda i, j: (0, 0)),
                  pl.BlockSpec((D_MODEL, tn), lambda i, j: (0, first_block + j)),
                  pl.BlockSpec((D_MODEL, LANES), lambda i, j: (0, 0))],
        out_specs=[pl.BlockSpec((tm, tn), lambda i, j: (i, j)),
                   pl.BlockSpec((tm, LANES), lambda i, j: (i, 0))],
        out_shape=[jax.ShapeDtypeStruct((t, n_blocks * tn), F32),
                   jax.ShapeDtypeStruct((t, LANES), F32)],
        scratch_shapes=[pltpu.VMEM((tm, D_MODEL), BF16)],
        compiler_params=_params(("arbitrary", "arbitrary")),
        name="inproj",
    )(x2d, shift, scale, g, w, w_dt)


def _conv_silu(cur_ref, r0, prev_rows, next_rows, w_ref, b_ref, pad_ref, sub):
    outs = []
    for s in range(XBC_DIM // LANES):
        ls = slice(s * LANES, (s + 1) * LANES)
        pad_ref[sub, s, 0:HALO, :] = prev_rows(ls)
        pad_ref[sub, s, HALO:HALO + CHUNK, :] = cur_ref[r0:r0 + CHUNK, ls]
        pad_ref[sub, s, HALO + CHUNK:PAD_ROWS, :] = next_rows(ls)
        acc = jnp.broadcast_to(b_ref[:, ls], (CHUNK, LANES))
        for k in range(CONV_W):
            start = HALO - CONV_W // 2 + k
            acc = acc + pad_ref[sub, s, pl.ds(start, CHUNK, stride=1), :] * w_ref[k:k + 1, ls]
        outs.append(_silu(acc))
    return jnp.concatenate(outs, axis=1)


def _dt_terms(dt_raw, bias_row, a_row, tl, tu):
    dt = _softplus(dt_raw + bias_row)
    a = dt * a_row
    s = jnp.dot(tl, a, precision=lax.Precision.HIGHEST, preferred_element_type=F32)
    r = jnp.dot(tu, a, precision=lax.Precision.HIGHEST, preferred_element_type=F32)
    is_fwd = lax.broadcasted_iota(jnp.int32, (CHUNK, LANES), 1) < SSD_HEADS
    cum = jnp.where(is_fwd, s, r)
    total = jnp.where(is_fwd[0:1], s[CHUNK - 1:CHUNK], r[0:1])
    return dt, cum, total


def _expand(v, e):
    hi = v.astype(BF16)
    lo = (v - hi.astype(F32)).astype(BF16)
    return _dot(hi, e) + _dot(lo, e)


PROJ_COLS = 256
CHUNKS_PER_STEP = 2
STEP_ROWS = CHUNKS_PER_STEP * CHUNK


def _scan_kernel(xc_ref, xp_ref, xn_ref, dt_ref, h0f_ref, h0b_ref, cw_ref, cb_ref, bias_ref,
                 arow_ref, tl_ref, tu_ref, ef_ref, eb_ref, act_ref, dto_ref, cum_ref, pf_ref, pb_ref,
                 ff_ref, fb_ref, pad_ref, sf_ref, sball_ref, dball_ref, *, nc):
    c = pl.program_id(1)
    nsteps = nc // CHUNKS_PER_STEP

    @pl.when(c == 0)
    def _():
        sf_ref[...] = h0f_ref[...]

    expand_in = []
    for sub in range(CHUNKS_PER_STEP):
        r0 = sub * CHUNK
        dt, cum, total = _dt_terms(dt_ref[r0:r0 + CHUNK, :], bias_ref[...], arow_ref[...],
                                   tl_ref[...], tu_ref[...])
        dto_ref[r0:r0 + CHUNK, :] = dt
        cum_ref[r0:r0 + CHUNK, :] = cum
        w = dt * jnp.exp(total - cum)
        decay = jnp.broadcast_to(jnp.exp(total), (SUBLANES, LANES))
        expand_in.append(jnp.concatenate([w, decay], axis=0))

    for sub in range(CHUNKS_PER_STEP):
        r0 = sub * CHUNK
        if sub == 0:
            prev_rows = lambda ls: jnp.where(c == 0, 0.0, xp_ref[:, ls])
        else:
            prev_rows = lambda ls, r0=r0: xc_ref[r0 - HALO:r0, ls]
        if sub == CHUNKS_PER_STEP - 1:
            next_rows = lambda ls: jnp.where(c == nsteps - 1, 0.0, xn_ref[:, ls])
        else:
            next_rows = lambda ls, r0=r0: xc_ref[r0 + CHUNK:r0 + CHUNK + HALO, ls]
        act_ref[r0:r0 + CHUNK, :] = _conv_silu(xc_ref, r0, prev_rows, next_rows, cw_ref, cb_ref,
                                               pad_ref, sub)

    state = sf_ref[...]
    for sub in range(CHUNKS_PER_STEP):
        r0 = sub * CHUNK
        v = expand_in[sub]
        xs = act_ref[r0:r0 + CHUNK, 0:SSD_DIM]
        bt = []
        for g in range(SSD_GROUPS):
            lo = SSD_DIM + g * SSD_STATE
            bt.append(act_ref[r0:r0 + CHUNK, lo:lo + SSD_STATE].T.astype(BF16))

        def chunk_state(e_ref):
            ex = _expand(v, e_ref[...])
            xw = (xs * ex[0:CHUNK]).astype(BF16)
            new = [_dot(bt[g], xw[:, g * GROUP_DIM:(g + 1) * GROUP_DIM])
                   for g in range(SSD_GROUPS)]
            return jnp.concatenate(new, axis=1), ex[CHUNK:CHUNK + SUBLANES]

        new_f, decay_f = chunk_state(ef_ref)
        pf_ref[sub] = state.astype(BF16)
        state = state * decay_f[0:1] + new_f

        new_b, decay_b = chunk_state(eb_ref)
        sball_ref[c * CHUNKS_PER_STEP + sub] = new_b
        dball_ref[c * CHUNKS_PER_STEP + sub] = decay_b
    sf_ref[...] = state

    @pl.when(c == nsteps - 1)
    def _():
        ff_ref[...] = sf_ref[...]
        sb = h0b_ref[...]
        for cc in reversed(range(nc)):
            pb_ref[cc] = sb.astype(BF16)
            sb = sb * dball_ref[cc, 0:1, :] + sball_ref[cc]
        fb_ref[...] = sb


def _halo_specs(nsteps, col):
    per = STEP_ROWS // HALO
    cur = pl.BlockSpec((STEP_ROWS, XBC_DIM), lambda b, c: (b * nsteps + c, col))
    prev = pl.BlockSpec((HALO, XBC_DIM),
                        lambda b, c: (jnp.maximum((b * nsteps + c) * per - 1, 0), col))
    nxt = pl.BlockSpec((HALO, XBC_DIM),
                       lambda b, c: (jnp.minimum((b * nsteps + c + 1) * per,
                                                 (b * nsteps + nsteps) * per - 1), col))
    return cur, prev, nxt


def _full_spec(a):
    return pl.BlockSpec(a.shape, lambda b, c: (0,) * a.ndim)


def _scan(p, dt, h0f, h0b, consts, *, nb, nc, col):
    assert nc % CHUNKS_PER_STEP == 0
    nsteps = nc // CHUNKS_PER_STEP
    state_spec = pl.BlockSpec((None, SSD_STATE, SSD_DIM), lambda b, c: (b, 0, 0))
    step_rows = lambda width: pl.BlockSpec((STEP_ROWS, width), lambda b, c: (b * nsteps + c, 0))
    args = (p, p, p, dt, h0f, h0b) + tuple(consts)
    in_specs = list(_halo_specs(nsteps, col)) + [step_rows(LANES), state_spec, state_spec]
    in_specs += [_full_spec(a) for a in consts]
    return pl.pallas_call(
        functools.partial(_scan_kernel, nc=nc),
        grid=(nb, nsteps),
        in_specs=in_specs,
        out_specs=[step_rows(XBC_DIM), step_rows(LANES), step_rows(LANES),
                   pl.BlockSpec((CHUNKS_PER_STEP, SSD_STATE, SSD_DIM),
                                lambda b, c: (b * nsteps + c, 0, 0)),
                   pl.BlockSpec((nc, SSD_STATE, SSD_DIM), lambda b, c: (b, 0, 0)),
                   state_spec, state_spec],
        out_shape=[jax.ShapeDtypeStruct((nb * nc * CHUNK, XBC_DIM), F32),
                   jax.ShapeDtypeStruct((nb * nc * CHUNK, LANES), F32),
                   jax.ShapeDtypeStruct((nb * nc * CHUNK, LANES), F32),
                   jax.ShapeDtypeStruct((nb * nc, SSD_STATE, SSD_DIM), BF16),
                   jax.ShapeDtypeStruct((nb * nc, SSD_STATE, SSD_DIM), BF16),
                   jax.ShapeDtypeStruct((nb, SSD_STATE, SSD_DIM), F32),
                   jax.ShapeDtypeStruct((nb, SSD_STATE, SSD_DIM), F32)],
        scratch_shapes=[pltpu.VMEM((CHUNKS_PER_STEP, XBC_DIM // LANES, PAD_ROWS, LANES), F32),
                        pltpu.VMEM((SSD_STATE, SSD_DIM), F32),
                        pltpu.VMEM((nc, SSD_STATE, SSD_DIM), F32),
                        pltpu.VMEM((nc, SUBLANES, SSD_DIM), F32)],
        compiler_params=_params(("arbitrary", "arbitrary")),
        name="ssd_scan",
    )(*args)


def _mixer_kernel(u_ref, v_ref, z_ref, act_ref, dt_ref, cum_ref, pf_ref, pb_ref,
                  x_ref, gate_ref, wout_ref, ef_ref, eb_ref,
                  dskip_ref, ng_ref, lg_ref, lb_ref, ws_ref, bs_ref,
                  o_ref, mix_ref, stage_ref):
    @pl.when(pl.program_id(0) == 0)
    def _():
        mix_ref[...] = jnp.zeros_like(mix_ref)

    def proj_piece(n):
        def run():
            cols = slice(n * PROJ_COLS, (n + 1) * PROJ_COLS)
            f = _dot(mix_ref[...], wout_ref[:, cols])
            o_ref[:, cols] = x_ref[:, cols] + gate_ref[:, cols] * f
        return run

    pieces = [proj_piece(n) for n in range(D_MODEL // PROJ_COLS)]
    per_chunk = len(pieces) // CHUNKS_PER_STEP
    for sub in range(CHUNKS_PER_STEP):
        mine = pieces[sub * per_chunk:(sub + 1) * per_chunk]
        _mixer_chunk(sub, u_ref, v_ref, z_ref, act_ref, dt_ref, cum_ref, pf_ref, pb_ref, ef_ref,
                     eb_ref, dskip_ref, ng_ref, lg_ref, lb_ref, ws_ref, bs_ref, stage_ref, mine)
        while mine:
            mine.pop(0)()
    mix_ref[...] = stage_ref[...]


def _mixer_chunk(sub, u_ref, v_ref, z_ref, act_ref, dt_ref, cum_ref, pf_ref, pb_ref, ef_ref, eb_ref,
                 dskip_ref, ng_ref, lg_ref, lb_ref, ws_ref, bs_ref, o_ref, fillers):
    def fill():
        if fillers:
            fillers.pop(0)()

    rows = slice(sub * CHUNK, (sub + 1) * CHUNK)
    xs = act_ref[rows, 0:SSD_DIM]
    gn = SSD_GROUPS * SSD_STATE
    bm = act_ref[rows, SSD_DIM:SSD_DIM + gn].astype(BF16)
    cm = act_ref[rows, SSD_DIM + gn:XBC_DIM].astype(BF16)

    dt = dt_ref[rows, :]
    cum = cum_ref[rows, :]
    ecum = jnp.exp(cum)
    ecum_f = _expand(ecum, ef_ref[...])
    ecum_b = _expand(ecum, eb_ref[...])
    dt_t = dt.T
    cum_t = cum.T

    lane = lax.broadcasted_iota(jnp.int32, (CHUNK, SSD_DIM), 1)
    first_head = (lane % LANES) < SSD_HEAD_DIM
    xs_even = jnp.where(first_head, xs, 0.0).astype(BF16)
    xs_odd = jnp.where(first_head, 0.0, xs).astype(BF16)

    row = lax.broadcasted_iota(jnp.int32, (CHUNK, CHUNK), 0)
    col = lax.broadcasted_iota(jnp.int32, (CHUNK, CHUNK), 1)
    causal = row >= col
    anticausal = row <= col

    def decay_matrix(h):
        lf = jnp.exp(jnp.minimum(cum[:, h:h + 1] - cum_t[h:h + 1, :], 0.0))
        lf = jnp.where(causal, lf, 0.0) * dt_t[h:h + 1, :]
        hb = SSD_HEADS + h
        lb = jnp.exp(jnp.minimum(cum[:, hb:hb + 1] - cum_t[hb:hb + 1, :], 0.0))
        lb = jnp.where(anticausal, lb, 0.0) * dt_t[hb:hb + 1, :]
        return lf + lb

    pf = pf_ref[sub]
    pb = pb_ref[sub]
    heads_per_group = SSD_HEADS // SSD_GROUPS
    y_parts = []
    for g in range(SSD_GROUPS):
        cg = cm[:, g * SSD_STATE:(g + 1) * SSD_STATE]
        bg = bm[:, g * SSD_STATE:(g + 1) * SSD_STATE]
        cb_g = lax.dot_general(cg, bg, (((1,), (1,)), ((), ())), preferred_element_type=F32)
        gs = slice(g * GROUP_DIM, (g + 1) * GROUP_DIM)
        y_off = ecum_f[:, gs] * _dot(cg, pf[:, gs]) + ecum_b[:, gs] * _dot(cg, pb[:, gs])
        diag = []
        for j in range(heads_per_group // 2):
            if j == heads_per_group // 4:
                fill()
            h0 = g * heads_per_group + 2 * j
            m_pair = jnp.concatenate([(cb_g * decay_matrix(h0)).astype(BF16),
                                      (cb_g * decay_matrix(h0 + 1)).astype(BF16)], axis=1)
            ls = slice(h0 * SSD_HEAD_DIM, (h0 + 2) * SSD_HEAD_DIM)
            rhs = jnp.concatenate([xs_even[:, ls], xs_odd[:, ls]], axis=0)
            diag.append(_dot(m_pair, rhs))
        y_parts.append(jnp.concatenate(diag, axis=1) + y_off)
    y = jnp.concatenate(y_parts, axis=1) + dskip_ref[...] * xs

    yz = y * _silu(z_ref[rows, :])
    normed = []
    for g in range(SSD_GROUPS):
        part = yz[:, g * GROUP_DIM:(g + 1) * GROUP_DIM]
        normed.append(part * lax.rsqrt(jnp.mean(part * part, axis=-1, keepdims=True) + EPS))
    ssd_out = jnp.concatenate(normed, axis=1) * ng_ref[...]
    o_ref[rows, A_DIM:A_DIM + SSD_DIM] = ssd_out.astype(BF16)

    fill()
    v = _gelu(v_ref[rows, :])
    vc = v - jnp.mean(v, axis=-1, keepdims=True)
    vn = vc * lax.rsqrt(jnp.mean(vc * vc, axis=-1, keepdims=True) + EPS) * lg_ref[...] + lb_ref[...]
    vn = vn.astype(BF16)
    for h in range(A_HEADS):
        if h == A_HEADS // 2:
            fill()
        hs = slice(h * A_HEAD_DIM, (h + 1) * A_HEAD_DIM)
        s = _dot(ws_ref[h], vn[:, hs]) + bs_ref[:, hs]
        o_ref[rows, hs] = (_gelu(u_ref[rows, hs]) * s).astype(BF16)


def _mixer(p, act, dt, cum, prev_f, prev_b, x2d, gate, w_out, expanders, extra, *, nb, nc):
    steps_per_batch = nc // CHUNKS_PER_STEP
    n_blocks = nb * steps_per_batch
    cur = lambda t: jnp.minimum(t, n_blocks - 1)
    lag = lambda t: jnp.maximum(t - 1, 0)
    lane_blk = lambda k: pl.BlockSpec((STEP_ROWS, A_DIM), lambda t: (cur(t), k))
    cur_rows = lambda width: pl.BlockSpec((STEP_ROWS, width), lambda t: (cur(t), 0))
    lag_rows = pl.BlockSpec((STEP_ROWS, D_MODEL), lambda t: (lag(t), 0))
    lag_mod = pl.BlockSpec((None, 1, D_MODEL), lambda t: (lag(t) // steps_per_batch, 0, 0))
    state_spec = pl.BlockSpec((CHUNKS_PER_STEP, SSD_STATE, SSD_DIM), lambda t: (cur(t), 0, 0))
    whole = lambda a: pl.BlockSpec(a.shape, lambda t: (0,) * a.ndim)
    params = tuple(expanders) + tuple(extra)
    args = (p, p, p, act, dt, cum, prev_f, prev_b, x2d, gate, w_out) + params
    in_specs = [lane_blk(0), lane_blk(1), lane_blk(2), cur_rows(XBC_DIM), cur_rows(LANES),
                cur_rows(LANES), state_spec, state_spec, lag_rows, lag_mod, whole(w_out)]
    in_specs += [whole(a) for a in params]
    t_rows = n_blocks * STEP_ROWS
    return pl.pallas_call(
        _mixer_kernel,
        grid=(n_blocks + 1,),
        in_specs=in_specs,
        out_specs=lag_rows,
        out_shape=jax.ShapeDtypeStruct((t_rows, D_MODEL), F32),
        scratch_shapes=[pltpu.VMEM((STEP_ROWS, A_DIM + SSD_DIM), BF16),
                        pltpu.VMEM((STEP_ROWS, A_DIM + SSD_DIM), BF16)],
        compiler_params=_params(("arbitrary",)),
        name="mixer",
    )(*args)


def _ffn_weights(w_gate, w_up, w_down):
    return w_gate.astype(BF16), w_up.astype(BF16), w_down.astype(BF16)


def _ssd_consts(conv_w, conv_b, dt_bias, a_log):
    lanes_pad = LANES - 2 * SSD_HEADS
    bias_row = jnp.pad(dt_bias.reshape(1, 2 * SSD_HEADS), ((0, 0), (0, lanes_pad)))
    a_row = jnp.pad(-jnp.exp(a_log.reshape(1, 2 * SSD_HEADS)), ((0, 0), (0, lanes_pad)))
    tl = jnp.asarray(np.tril(np.ones((CHUNK, CHUNK), np.float32)))
    head_of_lane = np.arange(SSD_DIM) // SSD_HEAD_DIM
    sel = np.arange(LANES)[:, None] == head_of_lane[None, :]
    sel_b = np.arange(LANES)[:, None] == (head_of_lane[None, :] + SSD_HEADS)
    ef = jnp.asarray(sel.astype(np.float32)).astype(BF16)
    eb = jnp.asarray(sel_b.astype(np.float32)).astype(BF16)
    return (conv_w, conv_b.reshape(1, XBC_DIM), bias_row, a_row, tl, tl.T, ef, eb)


def kernel(x, c, ctx, c_ctx, w_mod, b_mod, norm_ffn1, ffn1_w_gate, ffn1_w_up, ffn1_w_down,
           norm_mix, w_in, conv_w, conv_b, dt_bias, a_log, d_skip, ssd_norm_g, gmlp_norm_g,
           gmlp_norm_b, gmlp_w_s, gmlp_b_s, w_out, norm_ffn2, ffn2_w_gate, ffn2_w_up,
           ffn2_w_down, norm_final):
    assert w_mod.shape[0] == 1, "single-layer block"
    bsz, seq, _ = x.shape
    ctx_len = ctx.shape[1]
    nc_x = seq // CHUNK
    nc_c = ctx_len // CHUNK

    pad_rows = (-(bsz + 1)) % SUBLANES
    c_all = jnp.concatenate([c, c_ctx[None, :], jnp.zeros((pad_rows, D_MODEL), F32)], axis=0)
    mod = _modulation(c_all, w_mod[0], b_mod[0]).reshape(-1, N_MOD, D_MODEL)
    mod_x = [mod[:bsz, k][:, None, :] for k in range(N_MOD)]
    mod_c = [mod[bsz:bsz + 1, k][:, None, :] for k in range(N_MOD)]

    row = lambda a: a.reshape(1, -1)
    x2d = x.reshape(bsz * seq, D_MODEL)
    c2d = ctx.reshape(bsz * ctx_len, D_MODEL)

    w1 = _ffn_weights(ffn1_w_gate[0], ffn1_w_up[0], ffn1_w_down[0])
    g1 = row(norm_ffn1[0])
    x1 = _ffn(x2d, mod_x[0], mod_x[1], mod_x[2], g1, *w1, g1, rows_per_mod=seq,
              final_norm=False)
    c1 = _ffn(c2d, mod_c[0], mod_c[1], mod_c[2], g1, *w1, g1, rows_per_mod=bsz * ctx_len,
              final_norm=False)

    w_proj = w_in[0].astype(BF16)
    w_dt = jnp.pad(w_proj[:, MAIN_COLS:], ((0, 0), (0, LANES - 2 * SSD_HEADS)))
    gm = row(norm_mix[0])
    main_blocks = MAIN_COLS // XBC_DIM
    px, dtx = _inproj(x1, mod_x[3], mod_x[4], gm, w_proj, w_dt, rows_per_mod=seq, first_block=0,
                      n_blocks=main_blocks)
    pc, dtc = _inproj(c1, mod_c[3], mod_c[4], gm, w_proj, w_dt, rows_per_mod=bsz * ctx_len,
                      first_block=main_blocks - 1, n_blocks=1)

    consts = _ssd_consts(conv_w[0], conv_b[0], dt_bias[0], a_log[0])
    h0 = jnp.zeros((bsz, SSD_STATE, SSD_DIM), F32)
    *_, hf, hb = _scan(pc, dtc, h0, h0, consts, nb=bsz, nc=nc_c, col=0)
    act, dts, cum, prev_f, prev_b, _, _ = _scan(px, dtx, hf, hb, consts, nb=bsz, nc=nc_x,
                                                col=(MAIN_COLS - XBC_DIM) // XBC_DIM)

    extra = (row(jnp.repeat(d_skip[0], SSD_HEAD_DIM)), row(ssd_norm_g[0]), row(gmlp_norm_g[0]),
             row(gmlp_norm_b[0]), gmlp_w_s[0].astype(BF16),
             jnp.repeat(gmlp_b_s[0].T, A_HEAD_DIM, axis=1))
    x2 = _mixer(px, act, dts, cum, prev_f, prev_b, x1, mod_x[5], w_out[0].astype(BF16), consts[-2:],
                extra, nb=bsz, nc=nc_x)

    w2 = _ffn_weights(ffn2_w_gate[0], ffn2_w_up[0], ffn2_w_down[0])
    out = _ffn(x2, mod_x[6], mod_x[7], mod_x[8], row(norm_ffn2[0]), *w2, row(norm_final),
               rows_per_mod=seq, final_norm=True)
    return out.reshape(bsz, seq, D_MODEL)
```

```python
import functools
import math

import jax
import jax.numpy as jnp
import numpy as np
from jax import lax
from jax.experimental import pallas as pl
from jax.experimental.pallas import tpu as pltpu

F32 = jnp.float32
BF16 = jnp.bfloat16

D_MODEL = 2048
N_MOD = 9
D_FF = 5504
A_HEADS = 8
A_HEAD_DIM = 128
A_DIM = A_HEADS * A_HEAD_DIM
SSD_HEADS = 16
SSD_HEAD_DIM = 64
SSD_DIM = SSD_HEADS * SSD_HEAD_DIM
SSD_GROUPS = 2
SSD_STATE = 128
GROUP_DIM = SSD_DIM // SSD_GROUPS
CHUNK = 128
CONV_W = 5
XBC_DIM = SSD_DIM + 2 * SSD_GROUPS * SSD_STATE
MAIN_COLS = 2 * A_DIM + SSD_DIM + XBC_DIM
EPS = 1e-6

LANES = 128
SUBLANES = 8
VMEM_LIMIT_BYTES = 60 * 1024 * 1024

HALO = SUBLANES
PAD_ROWS = CHUNK + 2 * HALO


def _params(sem):
    return pltpu.CompilerParams(dimension_semantics=sem, vmem_limit_bytes=VMEM_LIMIT_BYTES)


def _dot(a, b):
    return jnp.dot(a, b, preferred_element_type=F32)


def _silu(x):
    return x * jax.nn.sigmoid(x)


def _gelu(x):
    return 0.5 * x * (1.0 + lax.erf(x * (1.0 / math.sqrt(2.0))))


def _softplus(x):
    return jnp.maximum(x, 0.0) + jnp.log1p(jnp.exp(-jnp.abs(x)))


def _rms(x, g):
    return x * lax.rsqrt(jnp.mean(x * x, axis=-1, keepdims=True) + EPS) * g


def _mod_kernel(c_ref, w_ref, b_ref, o_ref):
    o_ref[...] = _dot(_silu(c_ref[...]), w_ref[...]) + b_ref[...]


def _modulation(c_all, w_mod, b_mod, tn=1024):
    rows = c_all.shape[0]
    n = w_mod.shape[1]
    return pl.pallas_call(
        _mod_kernel,
        grid=(n // tn,),
        in_specs=[pl.BlockSpec((rows, D_MODEL), lambda j: (0, 0)),
                  pl.BlockSpec((D_MODEL, tn), lambda j: (0, j)),
                  pl.BlockSpec((1, tn), lambda j: (0, j))],
        out_specs=pl.BlockSpec((rows, tn), lambda j: (0, j)),
        out_shape=jax.ShapeDtypeStruct((rows, n), F32),
        compiler_params=_params(("arbitrary",)),
        name="mod",
    )(c_all, w_mod, b_mod.reshape(1, n))


FFN_TM = 1024
FFN_TF = 512


def _norm_mod(x, g, shift, scale):
    return (_rms(x, g) * (1.0 + scale) + shift).astype(BF16)


def _ffn_kernel(x_ref, shift_ref, scale_ref, gate_ref, g_ref, wg_ref, wu_ref, wd_ref, gf_ref,
                o_ref, h_ref, *, final_norm, last_cols):
    j = pl.program_id(1)
    last = pl.num_programs(1) - 1

    @pl.when(j == 0)
    def _():
        h_ref[...] = _norm_mod(x_ref[...], g_ref[...], shift_ref[...], scale_ref[...])

    def partial_ffn(cols):
        h = h_ref[...]
        gate = _dot(h, wg_ref[:, 0:cols])
        up = _dot(h, wu_ref[:, 0:cols])
        a = (_silu(gate) * up).astype(BF16)
        return _dot(a, wd_ref[0:cols, :])

    @pl.when(j == 0)
    def _():
        o_ref[...] = partial_ffn(FFN_TF)

    @pl.when((j > 0) & (j < last))
    def _():
        o_ref[...] += partial_ffn(FFN_TF)

    @pl.when(j == last)
    def _():
        f = o_ref[...] + partial_ffn(last_cols)
        r = x_ref[...] + (0.5 * gate_ref[...]) * f
        if final_norm:
            r = _rms(r, gf_ref[...])
        o_ref[...] = r


def _ffn(x2d, shift, scale, gate, g, wg, wu, wd, gf, *, rows_per_mod, final_norm):
    t = x2d.shape[0]
    tm, tf = min(FFN_TM, rows_per_mod), FFN_TF
    nf = pl.cdiv(D_FF, tf)
    assert nf >= 3 and t % tm == 0 and rows_per_mod % tm == 0
    tiles_per_mod = rows_per_mod // tm
    mod_spec = pl.BlockSpec((None, 1, D_MODEL), lambda i, j: (i // tiles_per_mod, 0, 0))
    row_spec = pl.BlockSpec((1, D_MODEL), lambda i, j: (0, 0))
    tile_spec = pl.BlockSpec((tm, D_MODEL), lambda i, j: (i, 0))
    w_specs = [pl.BlockSpec((D_MODEL, tf), lambda i, j: (0, j)),
               pl.BlockSpec((D_MODEL, tf), lambda i, j: (0, j)),
               pl.BlockSpec((tf, D_MODEL), lambda i, j: (j, 0))]
    args = (x2d, shift, scale, gate, g, wg, wu, wd, gf)
    in_specs = [tile_spec, mod_spec, mod_spec, mod_spec, row_spec] + w_specs + [row_spec]
    return pl.pallas_call(
        functools.partial(_ffn_kernel, final_norm=final_norm, last_cols=D_FF - (nf - 1) * tf),
        grid=(t // tm, nf),
        in_specs=in_specs,
        out_specs=tile_spec,
        out_shape=jax.ShapeDtypeStruct((t, D_MODEL), F32),
        scratch_shapes=[pltpu.VMEM((tm, D_MODEL), BF16)],
        compiler_params=_params(("arbitrary", "arbitrary")),
        name="ffn_final" if final_norm else "ffn",
    )(*args)


def _inproj_kernel(x_ref, shift_ref, scale_ref, g_ref, w_ref, wdt_ref, p_ref, dt_ref, h_ref):
    @pl.when(pl.program_id(1) == 0)
    def _():
        h_ref[...] = _norm_mod(x_ref[...], g_ref[...], shift_ref[...], scale_ref[...])
        p_ref[...] = _dot(h_ref[...], w_ref[...])
        dt_ref[...] = _dot(h_ref[...], wdt_ref[...])

    @pl.when(pl.program_id(1) > 0)
    def _():
        p_ref[...] = _dot(h_ref[...], w_ref[...])


def _inproj(x2d, shift, scale, g, w, w_dt, *, rows_per_mod, first_block, n_blocks, tn=XBC_DIM):
    t = x2d.shape[0]
    tm = min(1024, rows_per_mod)
    tiles_per_mod = rows_per_mod // tm
    mod_spec = pl.BlockSpec((None, 1, D_MODEL), lambda i, j: (i // tiles_per_mod, 0, 0))
    return pl.pallas_call(
        _inproj_kernel,
        grid=(t // tm, n_blocks),
        in_specs=[pl.BlockSpec((tm, D_MODEL), lambda i, j: (i, 0)), mod_spec, mod_spec,
                  pl.BlockSpec((1, D_MODEL), lambda i, j: (0, 0)),
                  pl.BlockSpec((D_MODEL, tn), lambda i, j: (0, first_block + j)),
                  pl.BlockSpec((D_MODEL, LANES), lambda i, j: (0, 0))],
        out_specs=[pl.BlockSpec((tm, tn), lambda i, j: (i, j)),
                   pl.BlockSpec((tm, LANES), lambda i, j: (i, 0))],
        out_shape=[jax.ShapeDtypeStruct((t, n_blocks * tn), F32),
                   jax.ShapeDtypeStruct((t, LANES), F32)],
        scratch_shapes=[pltpu.VMEM((tm, D_MODEL), BF16)],
        compiler_params=_params(("arbitrary", "arbitrary")),
        name="inproj",
    )(x2d, shift, scale, g, w, w_dt)


def _conv_silu(cur_ref, r0, prev_rows, next_rows, w_ref, b_ref, pad_ref, sub):
    outs = []
    for s in range(XBC_DIM // LANES):
        ls = slice(s * LANES, (s + 1) * LANES)
        pad_ref[sub, s, 0:HALO, :] = prev_rows(ls)
        pad_ref[sub, s, HALO:HALO + CHUNK, :] = cur_ref[r0:r0 + CHUNK, ls]
        pad_ref[sub, s, HALO + CHUNK:PAD_ROWS, :] = next_rows(ls)
        acc = jnp.broadcast_to(b_ref[:, ls], (CHUNK, LANES))
        for k in range(CONV_W):
            start = HALO - CONV_W // 2 + k
            acc = acc + pad_ref[sub, s, pl.ds(start, CHUNK, stride=1), :] * w_ref[k:k + 1, ls]
        outs.append(_silu(acc))
    return jnp.concatenate(outs, axis=1)


def _dt_terms(dt_raw, bias_row, a_row, tl, tu):
    dt = _softplus(dt_raw + bias_row)
    a = dt * a_row
    s = jnp.dot(tl, a, precision=lax.Precision.HIGHEST, preferred_element_type=F32)
    r = jnp.dot(tu, a, precision=lax.Precision.HIGHEST, preferred_element_type=F32)
    is_fwd = lax.broadcasted_iota(jnp.int32, (CHUNK, LANES), 1) < SSD_HEADS
    cum = jnp.where(is_fwd, s, r)
    total = jnp.where(is_fwd[0:1], s[CHUNK - 1:CHUNK], r[0:1])
    return dt, cum, total


def _expand(v, e):
    hi = v.astype(BF16)
    lo = (v - hi.astype(F32)).astype(BF16)
    return _dot(hi, e) + _dot(lo, e)


PROJ_COLS = 256
SCAN_CHUNKS = 4
CHUNKS_PER_STEP = 2
STEP_ROWS = CHUNKS_PER_STEP * CHUNK


def _scan_kernel(xc_ref, xp_ref, xn_ref, dt_ref, h0f_ref, h0b_ref, cw_ref, cb_ref, bias_ref,
                 arow_ref, tl_ref, tu_ref, ef_ref, eb_ref, act_ref, dto_ref, cum_ref, pf_ref, pb_ref,
                 ff_ref, fb_ref, pad_ref, sf_ref, sball_ref, dball_ref, *, nc, cps):
    c = pl.program_id(1)
    nsteps = nc // cps

    @pl.when(c == 0)
    def _():
        sf_ref[...] = h0f_ref[...]

    expand_in = []
    for sub in range(cps):
        r0 = sub * CHUNK
        dt, cum, total = _dt_terms(dt_ref[r0:r0 + CHUNK, :], bias_ref[...], arow_ref[...],
                                   tl_ref[...], tu_ref[...])
        dto_ref[r0:r0 + CHUNK, :] = dt
        cum_ref[r0:r0 + CHUNK, :] = cum
        w = dt * jnp.exp(total - cum)
        decay = jnp.broadcast_to(jnp.exp(total), (SUBLANES, LANES))
        expand_in.append(jnp.concatenate([w, decay], axis=0))

    for sub in range(cps):
        r0 = sub * CHUNK
        if sub == 0:
            prev_rows = lambda ls: jnp.where(c == 0, 0.0, xp_ref[:, ls])
        else:
            prev_rows = lambda ls, r0=r0: xc_ref[r0 - HALO:r0, ls]
        if sub == cps - 1:
            next_rows = lambda ls: jnp.where(c == nsteps - 1, 0.0, xn_ref[:, ls])
        else:
            next_rows = lambda ls, r0=r0: xc_ref[r0 + CHUNK:r0 + CHUNK + HALO, ls]
        act_ref[r0:r0 + CHUNK, :] = _conv_silu(xc_ref, r0, prev_rows, next_rows, cw_ref, cb_ref,
                                               pad_ref, sub)

    state = sf_ref[...]
    for sub in range(cps):
        r0 = sub * CHUNK
        v = expand_in[sub]
        xs = act_ref[r0:r0 + CHUNK, 0:SSD_DIM]
        bt = []
        for g in range(SSD_GROUPS):
            lo = SSD_DIM + g * SSD_STATE
            bt.append(act_ref[r0:r0 + CHUNK, lo:lo + SSD_STATE].T.astype(BF16))

        def chunk_state(e_ref):
            ex = _expand(v, e_ref[...])
            xw = (xs * ex[0:CHUNK]).astype(BF16)
            new = [_dot(bt[g], xw[:, g * GROUP_DIM:(g + 1) * GROUP_DIM])
                   for g in range(SSD_GROUPS)]
            return jnp.concatenate(new, axis=1), ex[CHUNK:CHUNK + SUBLANES]

        new_f, decay_f = chunk_state(ef_ref)
        pf_ref[sub] = state.astype(BF16)
        state = state * decay_f[0:1] + new_f

        new_b, decay_b = chunk_state(eb_ref)
        sball_ref[c * cps + sub] = new_b
        dball_ref[c * cps + sub] = decay_b
    sf_ref[...] = state

    @pl.when(c == nsteps - 1)
    def _():
        ff_ref[...] = sf_ref[...]
        sb = h0b_ref[...]
        for cc in reversed(range(nc)):
            pb_ref[cc] = sb.astype(BF16)
            sb = sb * dball_ref[cc, 0:1, :] + sball_ref[cc]
        fb_ref[...] = sb


def _halo_specs(nsteps, col, rows):
    per = rows // HALO
    cur = pl.BlockSpec((rows, XBC_DIM), lambda b, c: (b * nsteps + c, col))
    prev = pl.BlockSpec((HALO, XBC_DIM),
                        lambda b, c: (jnp.maximum((b * nsteps + c) * per - 1, 0), col))
    nxt = pl.BlockSpec((HALO, XBC_DIM),
                       lambda b, c: (jnp.minimum((b * nsteps + c + 1) * per,
                                                 (b * nsteps + nsteps) * per - 1), col))
    return cur, prev, nxt


def _full_spec(a):
    return pl.BlockSpec(a.shape, lambda b, c: (0,) * a.ndim)


def _scan(p, dt, h0f, h0b, consts, *, nb, nc, col):
    cps = min(SCAN_CHUNKS, nc)
    assert nc % cps == 0
    nsteps = nc // cps
    rows = cps * CHUNK
    state_spec = pl.BlockSpec((None, SSD_STATE, SSD_DIM), lambda b, c: (b, 0, 0))
    step_rows = lambda width: pl.BlockSpec((rows, width), lambda b, c: (b * nsteps + c, 0))
    args = (p, p, p, dt, h0f, h0b) + tuple(consts)
    in_specs = list(_halo_specs(nsteps, col, rows)) + [step_rows(LANES), state_spec, state_spec]
    in_specs += [_full_spec(a) for a in consts]
    return pl.pallas_call(
        functools.partial(_scan_kernel, nc=nc, cps=cps),
        grid=(nb, nsteps),
        in_specs=in_specs,
        out_specs=[step_rows(XBC_DIM), step_rows(LANES), step_rows(LANES),
                   pl.BlockSpec((cps, SSD_STATE, SSD_DIM),
                                lambda b, c: (b * nsteps + c, 0, 0)),
                   pl.BlockSpec((nc, SSD_STATE, SSD_DIM), lambda b, c: (b, 0, 0)),
                   state_spec, state_spec],
        out_shape=[jax.ShapeDtypeStruct((nb * nc * CHUNK, XBC_DIM), F32),
                   jax.ShapeDtypeStruct((nb * nc * CHUNK, LANES), F32),
                   jax.ShapeDtypeStruct((nb * nc * CHUNK, LANES), F32),
                   jax.ShapeDtypeStruct((nb * nc, SSD_STATE, SSD_DIM), BF16),
                   jax.ShapeDtypeStruct((nb * nc, SSD_STATE, SSD_DIM), BF16),
                   jax.ShapeDtypeStruct((nb, SSD_STATE, SSD_DIM), F32),
                   jax.ShapeDtypeStruct((nb, SSD_STATE, SSD_DIM), F32)],
        scratch_shapes=[pltpu.VMEM((cps, XBC_DIM // LANES, PAD_ROWS, LANES), F32),
                        pltpu.VMEM((SSD_STATE, SSD_DIM), F32),
                        pltpu.VMEM((nc, SSD_STATE, SSD_DIM), F32),
                        pltpu.VMEM((nc, SUBLANES, SSD_DIM), F32)],
        compiler_params=_params(("arbitrary", "arbitrary")),
        name="ssd_scan",
    )(*args)


def _mixer_kernel(u_ref, v_ref, z_ref, act_ref, dt_ref, cum_ref, pf_ref, pb_ref,
                  x_ref, gate_ref, wout_ref, ef_ref, eb_ref,
                  dskip_ref, ng_ref, lg_ref, lb_ref, ws_ref, bs_ref,
                  o_ref, mix_ref, stage_ref):
    @pl.when(pl.program_id(0) == 0)
    def _():
        mix_ref[...] = jnp.zeros_like(mix_ref)

    def proj_piece(n):
        def run():
            cols = slice(n * PROJ_COLS, (n + 1) * PROJ_COLS)
            f = _dot(mix_ref[...], wout_ref[:, cols])
            o_ref[:, cols] = x_ref[:, cols] + gate_ref[:, cols] * f
        return run

    pieces = [proj_piece(n) for n in range(D_MODEL // PROJ_COLS)]
    per_chunk = len(pieces) // CHUNKS_PER_STEP
    for sub in range(CHUNKS_PER_STEP):
        mine = pieces[sub * per_chunk:(sub + 1) * per_chunk]
        _mixer_chunk(sub, u_ref, v_ref, z_ref, act_ref, dt_ref, cum_ref, pf_ref, pb_ref, ef_ref,
                     eb_ref, dskip_ref, ng_ref, lg_ref, lb_ref, ws_ref, bs_ref, stage_ref, mine)
        while mine:
            mine.pop(0)()
    mix_ref[...] = stage_ref[...]


def _mixer_chunk(sub, u_ref, v_ref, z_ref, act_ref, dt_ref, cum_ref, pf_ref, pb_ref, ef_ref, eb_ref,
                 dskip_ref, ng_ref, lg_ref, lb_ref, ws_ref, bs_ref, o_ref, fillers):
    def fill():
        if fillers:
            fillers.pop(0)()

    rows = slice(sub * CHUNK, (sub + 1) * CHUNK)
    xs = act_ref[rows, 0:SSD_DIM]
    gn = SSD_GROUPS * SSD_STATE
    bm = act_ref[rows, SSD_DIM:SSD_DIM + gn].astype(BF16)
    cm = act_ref[rows, SSD_DIM + gn:XBC_DIM].astype(BF16)

    dt = dt_ref[rows, :]
    cum = cum_ref[rows, :]
    ecum = jnp.exp(cum)
    ecum_f = _expand(ecum, ef_ref[...])
    ecum_b = _expand(ecum, eb_ref[...])
    dt_t = dt.T
    cum_t = cum.T

    lane = lax.broadcasted_iota(jnp.int32, (CHUNK, SSD_DIM), 1)
    first_head = (lane % LANES) < SSD_HEAD_DIM
    xs_even = jnp.where(first_head, xs, 0.0).astype(BF16)
    xs_odd = jnp.where(first_head, 0.0, xs).astype(BF16)

    row = lax.broadcasted_iota(jnp.int32, (CHUNK, CHUNK), 0)
    col = lax.broadcasted_iota(jnp.int32, (CHUNK, CHUNK), 1)
    causal = row >= col
    anticausal = row <= col

    def decay_matrix(h):
        lf = jnp.exp(jnp.minimum(cum[:, h:h + 1] - cum_t[h:h + 1, :], 0.0))
        lf = jnp.where(causal, lf, 0.0) * dt_t[h:h + 1, :]
        hb = SSD_HEADS + h
        lb = jnp.exp(jnp.minimum(cum[:, hb:hb + 1] - cum_t[hb:hb + 1, :], 0.0))
        lb = jnp.where(anticausal, lb, 0.0) * dt_t[hb:hb + 1, :]
        return lf + lb

    pf = pf_ref[sub]
    pb = pb_ref[sub]
    heads_per_group = SSD_HEADS // SSD_GROUPS
    y_parts = []
    for g in range(SSD_GROUPS):
        cg = cm[:, g * SSD_STATE:(g + 1) * SSD_STATE]
        bg = bm[:, g * SSD_STATE:(g + 1) * SSD_STATE]
        cb_g = lax.dot_general(cg, bg, (((1,), (1,)), ((), ())), preferred_element_type=F32)
        gs = slice(g * GROUP_DIM, (g + 1) * GROUP_DIM)
        y_off = ecum_f[:, gs] * _dot(cg, pf[:, gs]) + ecum_b[:, gs] * _dot(cg, pb[:, gs])
        diag = []
        for j in range(heads_per_group // 2):
            if j == heads_per_group // 4:
                fill()
            h0 = g * heads_per_group + 2 * j
            m_pair = jnp.concatenate([(cb_g * decay_matrix(h0)).astype(BF16),
                                      (cb_g * decay_matrix(h0 + 1)).astype(BF16)], axis=1)
            ls = slice(h0 * SSD_HEAD_DIM, (h0 + 2) * SSD_HEAD_DIM)
            rhs = jnp.concatenate([xs_even[:, ls], xs_odd[:, ls]], axis=0)
            diag.append(_dot(m_pair, rhs))
        y_parts.append(jnp.concatenate(diag, axis=1) + y_off)
    y = jnp.concatenate(y_parts, axis=1) + dskip_ref[...] * xs

    yz = y * _silu(z_ref[rows, :])
    normed = []
    for g in range(SSD_GROUPS):
        part = yz[:, g * GROUP_DIM:(g + 1) * GROUP_DIM]
        normed.append(part * lax.rsqrt(jnp.mean(part * part, axis=-1, keepdims=True) + EPS))
    ssd_out = jnp.concatenate(normed, axis=1) * ng_ref[...]
    o_ref[rows, A_DIM:A_DIM + SSD_DIM] = ssd_out.astype(BF16)

    fill()
    v = _gelu(v_ref[rows, :])
    vc = v - jnp.mean(v, axis=-1, keepdims=True)
    vn = vc * lax.rsqrt(jnp.mean(vc * vc, axis=-1, keepdims=True) + EPS) * lg_ref[...] + lb_ref[...]
    vn = vn.astype(BF16)
    for h in range(A_HEADS):
        if h == A_HEADS // 2:
            fill()
        hs = slice(h * A_HEAD_DIM, (h + 1) * A_HEAD_DIM)
        s = _dot(ws_ref[h], vn[:, hs]) + bs_ref[:, hs]
        o_ref[rows, hs] = (_gelu(u_ref[rows, hs]) * s).astype(BF16)


def _mixer(p, act, dt, cum, prev_f, prev_b, x2d, gate, w_out, expanders, extra, *, nb, nc):
    steps_per_batch = nc // CHUNKS_PER_STEP
    n_blocks = nb * steps_per_batch
    cur = lambda t: jnp.minimum(t, n_blocks - 1)
    lag = lambda t: jnp.maximum(t - 1, 0)
    lane_blk = lambda k: pl.BlockSpec((STEP_ROWS, A_DIM), lambda t: (cur(t), k))
    cur_rows = lambda width: pl.BlockSpec((STEP_ROWS, width), lambda t: (cur(t), 0))
    lag_rows = pl.BlockSpec((STEP_ROWS, D_MODEL), lambda t: (lag(t), 0))
    lag_mod = pl.BlockSpec((None, 1, D_MODEL), lambda t: (lag(t) // steps_per_batch, 0, 0))
    state_spec = pl.BlockSpec((CHUNKS_PER_STEP, SSD_STATE, SSD_DIM), lambda t: (cur(t), 0, 0))
    whole = lambda a: pl.BlockSpec(a.shape, lambda t: (0,) * a.ndim)
    params = tuple(expanders) + tuple(extra)
    args = (p, p, p, act, dt, cum, prev_f, prev_b, x2d, gate, w_out) + params
    in_specs = [lane_blk(0), lane_blk(1), lane_blk(2), cur_rows(XBC_DIM), cur_rows(LANES),
                cur_rows(LANES), state_spec, state_spec, lag_rows, lag_mod, whole(w_out)]
    in_specs += [whole(a) for a in params]
    t_rows = n_blocks * STEP_ROWS
    return pl.pallas_call(
        _mixer_kernel,
        grid=(n_blocks + 1,),
        in_specs=in_specs,
        out_specs=lag_rows,
        out_shape=jax.ShapeDtypeStruct((t_rows, D_MODEL), F32),
        scratch_shapes=[pltpu.VMEM((STEP_ROWS, A_DIM + SSD_DIM), BF16),
                        pltpu.VMEM((STEP_ROWS, A_DIM + SSD_DIM), BF16)],
        compiler_params=_params(("arbitrary",)),
        name="mixer",
    )(*args)


def _ffn_weights(w_gate, w_up, w_down):
    return w_gate.astype(BF16), w_up.astype(BF16), w_down.astype(BF16)


def _ssd_consts(conv_w, conv_b, dt_bias, a_log):
    lanes_pad = LANES - 2 * SSD_HEADS
    bias_row = jnp.pad(dt_bias.reshape(1, 2 * SSD_HEADS), ((0, 0), (0, lanes_pad)))
    a_row = jnp.pad(-jnp.exp(a_log.reshape(1, 2 * SSD_HEADS)), ((0, 0), (0, lanes_pad)))
    tl = jnp.asarray(np.tril(np.ones((CHUNK, CHUNK), np.float32)))
    head_of_lane = np.arange(SSD_DIM) // SSD_HEAD_DIM
    sel = np.arange(LANES)[:, None] == head_of_lane[None, :]
    sel_b = np.arange(LANES)[:, None] == (head_of_lane[None, :] + SSD_HEADS)
    ef = jnp.asarray(sel.astype(np.float32)).astype(BF16)
    eb = jnp.asarray(sel_b.astype(np.float32)).astype(BF16)
    return (conv_w, conv_b.reshape(1, XBC_DIM), bias_row, a_row, tl, tl.T, ef, eb)


def kernel(x, c, ctx, c_ctx, w_mod, b_mod, norm_ffn1, ffn1_w_gate, ffn1_w_up, ffn1_w_down,
           norm_mix, w_in, conv_w, conv_b, dt_bias, a_log, d_skip, ssd_norm_g, gmlp_norm_g,
           gmlp_norm_b, gmlp_w_s, gmlp_b_s, w_out, norm_ffn2, ffn2_w_gate, ffn2_w_up,
           ffn2_w_down, norm_final):
    assert w_mod.shape[0] == 1, "single-layer block"
    bsz, seq, _ = x.shape
    ctx_len = ctx.shape[1]
    nc_x = seq // CHUNK
    nc_c = ctx_len // CHUNK

    pad_rows = (-(bsz + 1)) % SUBLANES
    c_all = jnp.concatenate([c, c_ctx[None, :], jnp.zeros((pad_rows, D_MODEL), F32)], axis=0)
    mod = _modulation(c_all, w_mod[0], b_mod[0]).reshape(-1, N_MOD, D_MODEL)
    mod_x = [mod[:bsz, k][:, None, :] for k in range(N_MOD)]
    mod_c = [mod[bsz:bsz + 1, k][:, None, :] for k in range(N_MOD)]

    row = lambda a: a.reshape(1, -1)
    x2d = x.reshape(bsz * seq, D_MODEL)
    c2d = ctx.reshape(bsz * ctx_len, D_MODEL)

    w1 = _ffn_weights(ffn1_w_gate[0], ffn1_w_up[0], ffn1_w_down[0])
    g1 = row(norm_ffn1[0])
    x1 = _ffn(x2d, mod_x[0], mod_x[1], mod_x[2], g1, *w1, g1, rows_per_mod=seq,
              final_norm=False)
    c1 = _ffn(c2d, mod_c[0], mod_c[1], mod_c[2], g1, *w1, g1, rows_per_mod=bsz * ctx_len,
              final_norm=False)

    w_proj = w_in[0].astype(BF16)
    w_dt = jnp.pad(w_proj[:, MAIN_COLS:], ((0, 0), (0, LANES - 2 * SSD_HEADS)))
    gm = row(norm_mix[0])
    main_blocks = MAIN_COLS // XBC_DIM
    px, dtx = _inproj(x1, mod_x[3], mod_x[4], gm, w_proj, w_dt, rows_per_mod=seq, first_block=0,
                      n_blocks=main_blocks)
    pc, dtc = _inproj(c1, mod_c[3], mod_c[4], gm, w_proj, w_dt, rows_per_mod=bsz * ctx_len,
                      first_block=main_blocks - 1, n_blocks=1)

    consts = _ssd_consts(conv_w[0], conv_b[0], dt_bias[0], a_log[0])
    h0 = jnp.zeros((bsz, SSD_STATE, SSD_DIM), F32)
    *_, hf, hb = _scan(pc, dtc, h0, h0, consts, nb=bsz, nc=nc_c, col=0)
    act, dts, cum, prev_f, prev_b, _, _ = _scan(px, dtx, hf, hb, consts, nb=bsz, nc=nc_x,
                                                col=(MAIN_COLS - XBC_DIM) // XBC_DIM)

    extra = (row(jnp.repeat(d_skip[0], SSD_HEAD_DIM)), row(ssd_norm_g[0]), row(gmlp_norm_g[0]),
             row(gmlp_norm_b[0]), gmlp_w_s[0].astype(BF16),
             jnp.repeat(gmlp_b_s[0].T, A_HEAD_DIM, axis=1))
    x2 = _mixer(px, act, dts, cum, prev_f, prev_b, x1, mod_x[5], w_out[0].astype(BF16), consts[-2:],
                extra, nb=bsz, nc=nc_x)

    w2 = _ffn_weights(ffn2_w_gate[0], ffn2_w_up[0], ffn2_w_down[0])
    out = _ffn(x2, mod_x[6], mod_x[7], mod_x[8], row(norm_ffn2[0]), *w2, row(norm_final),
               rows_per_mod=seq, final_norm=True)
    return out.reshape(bsz, seq, D_MODEL)
```
